```python
import math
import jax, jax.numpy as jnp
from jax import lax
import numpy as np

D_MODEL = 1024
BATCH = 2
SEQ = 8192
DEPTH = 2

N_MIXERS = 2
N_ATTN_LAYERS = (DEPTH + 1) // 2
N_SSD_LAYERS = DEPTH // 2

MEM_LEN = 256
EPS = 1e-6

DA_HEAD_DIM = 64
DA_N_HEADS = D_MODEL // (2 * DA_HEAD_DIM)
DA_QK_WIDTH = DA_N_HEADS * 2 * DA_HEAD_DIM
DA_V_WIDTH = DA_N_HEADS * 2 * DA_HEAD_DIM
DA_IN_WIDTH = 2 * DA_QK_WIDTH + DA_V_WIDTH
Q_BLOCK = 128
ROPE_THETA = 10000.0

SSD_EXPAND = 2
SSD_D_INNER = SSD_EXPAND * D_MODEL
SSD_HEAD_DIM = 64
SSD_N_HEADS = SSD_D_INNER // SSD_HEAD_DIM
SSD_N_GROUPS = 8
SSD_HEADS_PER_GROUP = SSD_N_HEADS // SSD_N_GROUPS
SSD_D_STATE = 128
SSD_CONV_WIDTH = 4
SSD_CONV_DIM = SSD_D_INNER + 2 * SSD_N_GROUPS * SSD_D_STATE
SSD_IN_WIDTH = SSD_D_INNER + SSD_CONV_DIM + SSD_N_HEADS
SSD_CHUNK = 128

XA_N_HEADS = 4
XA_HEAD_DIM = D_MODEL // XA_N_HEADS

FFN_HIDDEN = 2816

kernel_name = "hybrid_diffattn_mamba2_macaron_memxattn"


def rmsnorm(x, w):
    xf = x.astype(jnp.float32)
    y = xf * lax.rsqrt(jnp.mean(xf * xf, axis=-1, keepdims=True) + EPS)
    return (y * w.astype(jnp.float32)).astype(x.dtype)


def swiglu(h, w_gate, w_up, w_down):
    return (jax.nn.silu(h @ w_gate) * (h @ w_up)) @ w_down


def rope_tables(positions):
    inv = 1.0 / (ROPE_THETA ** (jnp.arange(0, DA_HEAD_DIM, 2, dtype=jnp.float32) / DA_HEAD_DIM))
    ang = positions.astype(jnp.float32)[..., None] * inv
    return jnp.cos(ang), jnp.sin(ang)


def apply_rope(x, cos, sin):
    xf = x.astype(jnp.float32)
    half = xf.shape[-1] // 2
    x1, x2 = xf[..., :half], xf[..., half:]
    c, s = cos[:, :, None, :], sin[:, :, None, :]
    return jnp.concatenate([x1 * c - x2 * s, x2 * c + x1 * s], axis=-1).astype(x.dtype)


def lambda_init_fn(layer_idx):
    return 0.8 - 0.6 * math.exp(-0.3 * layer_idx)


def diff_attention(h, cos, sin, w_in, lq1, lk1, lq2, lk2, subln, w_out, lambda_init):
    B, S, _ = h.shape
    H, d = DA_N_HEADS, DA_HEAD_DIM
    proj = h @ w_in
    q, k, v = jnp.split(proj, [DA_QK_WIDTH, 2 * DA_QK_WIDTH], axis=-1)
    q = apply_rope(q.reshape(B, S, 2 * H, d), cos, sin) * (d ** -0.5)
    k = apply_rope(k.reshape(B, S, 2 * H, d), cos, sin)
    q = q.reshape(B, S, H, 2, d)
    k = k.reshape(B, S, H, 2, d)
    v = v.reshape(B, S, H, 2 * d)
    lam = (jnp.exp(jnp.sum(lq1.astype(jnp.float32) * lk1.astype(jnp.float32)))
           - jnp.exp(jnp.sum(lq2.astype(jnp.float32) * lk2.astype(jnp.float32)))
           + lambda_init)
    n_blk = S // Q_BLOCK
    q_blocks = q.reshape(B, n_blk, Q_BLOCK, H, 2, d).transpose(1, 0, 2, 3, 4, 5)
    k_pos = jnp.arange(S)

    def block(args):
        qb, bi = args
        q_pos = bi * Q_BLOCK + jnp.arange(Q_BLOCK)
        s = jnp.einsum("bqhcd,bkhcd->bhcqk", qb, k).astype(jnp.float32)
        mask = k_pos[None, :] <= q_pos[:, None]
        s = jnp.where(mask, s, -jnp.inf)
        p = jax.nn.softmax(s, axis=-1)
        a = p[:, :, 0] - lam * p[:, :, 1]
        return jnp.einsum("bhqk,bkhe->bqhe", a.astype(v.dtype), v)

    o = lax.map(block, (q_blocks, jnp.arange(n_blk)))
    o = o.transpose(1, 0, 2, 3, 4).reshape(B, S, H, 2 * d)
    o = rmsnorm(o, subln) * (1.0 - lambda_init)
    return o.reshape(B, S, H * 2 * d) @ w_out


def causal_depthwise_conv(u, w, b):
    C = u.shape[-1]
    out = lax.conv_general_dilated(
        u, w[:, None, :], window_strides=(1,), padding=[(SSD_CONV_WIDTH - 1, 0)],
        dimension_numbers=("NWC", "WIO", "NWC"), feature_group_count=C)
    return out + b


def ssd_chunked_scan(x, a, Bm, Cm):
    Bsz, S, G, R, P = x.shape
    N = Bm.shape[-1]
    L = SSD_CHUNK
    c = S // L
    x = x.reshape(Bsz, c, L, G, R, P)
    Bm = Bm.reshape(Bsz, c, L, G, N)
    Cm = Cm.reshape(Bsz, c, L, G, N)
    a_cs = jnp.cumsum(a.reshape(Bsz, c, L, G, R), axis=2).transpose(0, 3, 4, 1, 2)
    seg = a_cs[..., :, None] - a_cs[..., None, :]
    tril = jnp.tril(jnp.ones((L, L), dtype=bool))
    decay = jnp.exp(jnp.where(tril, seg, -jnp.inf))
    cb = jnp.einsum("bclgn,bcsgn->bgcls", Cm, Bm)
    y_diag = jnp.einsum("bgcls,bgrcls,bcsgrp->bclgrp", cb, decay, x)
    decay_to_end = jnp.exp(a_cs[..., -1:] - a_cs)
    states = jnp.einsum("bclgn,bgrcl,bclgrp->cbgrpn", Bm, decay_to_end, x)
    chunk_decay = jnp.exp(a_cs[..., -1]).transpose(3, 0, 1, 2)

    def step(carry, inp):
        st, dec = inp
        return carry * dec[..., None, None] + st, carry

    init = jnp.zeros(states.shape[1:], dtype=states.dtype)
    _, prev_states = lax.scan(step, init, (states, chunk_decay))
    y_off = jnp.einsum("bclgn,cbgrpn,bgrcl->bclgrp", Cm, prev_states, jnp.exp(a_cs))
    return (y_diag + y_off).reshape(Bsz, S, G, R, P)


def ssd_mixer(h, w_in, conv_w, conv_b, dt_bias, A_log, D_skip, gnorm, w_out):
    B, S, _ = h.shape
    G, R, N, P = SSD_N_GROUPS, SSD_HEADS_PER_GROUP, SSD_D_STATE, SSD_HEAD_DIM
    proj = h @ w_in
    z, xbc, dt = jnp.split(proj, [SSD_D_INNER, SSD_D_INNER + SSD_CONV_DIM], axis=-1)
    xbc = jax.nn.silu(causal_depthwise_conv(xbc, conv_w, conv_b))
    xs, Bm, Cm = jnp.split(xbc, [SSD_D_INNER, SSD_D_INNER + G * N], axis=-1)
    xs = xs.reshape(B, S, G, R, P)
    Bm = Bm.reshape(B, S, G, N)
    Cm = Cm.reshape(B, S, G, N)
    dt = jax.nn.softplus((dt + dt_bias).astype(jnp.float32)).reshape(B, S, G, R)
    A = -jnp.exp(A_log.astype(jnp.float32)).reshape(G, R)
    y = ssd_chunked_scan(xs * dt[..., None], dt * A, Bm, Cm)
    y = y + D_skip.reshape(G, R)[:, :, None].astype(jnp.float32) * xs
    y = y.reshape(B, S, SSD_D_INNER) * jax.nn.silu(z.astype(jnp.float32))
    yg = y.reshape(B, S, G, SSD_D_INNER // G)
    yg = yg * lax.rsqrt(jnp.mean(yg * yg, axis=-1, keepdims=True) + EPS)
    y = (yg.reshape(B, S, SSD_D_INNER) * gnorm.astype(jnp.float32)).astype(h.dtype)
    return y @ w_out


def memory_cross_attention(h, mem, mem_norm, w_q, w_kv, w_o):
    B, S, _ = h.shape
    M = mem.shape[1]
    m = rmsnorm(mem, mem_norm)
    q = (h @ w_q).reshape(B, S, XA_N_HEADS, XA_HEAD_DIM)
    k, v = jnp.split(m @ w_kv, 2, axis=-1)
    k = k.reshape(B, M, XA_N_HEADS, XA_HEAD_DIM)
    v = v.reshape(B, M, XA_N_HEADS, XA_HEAD_DIM)
    s = jnp.einsum("bqhd,bmhd->bhqm", q, k).astype(jnp.float32) * (XA_HEAD_DIM ** -0.5)
    p = jax.nn.softmax(s, axis=-1)
    o = jnp.einsum("bhqm,bmhd->bqhd", p.astype(v.dtype), v).reshape(B, S, D_MODEL)
    return o @ w_o


def setup_inputs(seed: int = 0) -> dict:
    key = jax.random.key(seed)
    ks = iter(jax.random.split(key, 64))

    def w(shape, fan_in):
        return jax.random.normal(next(ks), shape, jnp.float32) * (fan_in ** -0.5)

    def gain(shape):
        return 1.0 + 0.02 * jax.random.normal(next(ks), shape, jnp.float32)

    def small(shape, scale):
        return scale * jax.random.normal(next(ks), shape, jnp.float32)

    x = jax.random.normal(next(ks), (BATCH, SEQ, D_MODEL), jnp.float32)
    mem = jax.random.normal(next(ks), (BATCH, MEM_LEN, D_MODEL), jnp.float32)
    positions = jnp.broadcast_to(jnp.arange(SEQ, dtype=jnp.int32), (BATCH, SEQ))

    dt0 = jnp.exp(jax.random.uniform(next(ks), (N_SSD_LAYERS, SSD_N_HEADS), jnp.float32)
                  * (math.log(0.1) - math.log(0.001)) + math.log(0.001))
    ssd_dt_bias = dt0 + jnp.log(-jnp.expm1(-dt0))
    ssd_A_log = jnp.log(jax.random.uniform(next(ks), (N_SSD_LAYERS, SSD_N_HEADS), jnp.float32,
                                           minval=1.0, maxval=16.0))

    return {
        "x": x,
        "mem": mem,
        "positions": positions,
        "ffn1_norm": gain((DEPTH, D_MODEL)),
        "ffn1_w_gate": w((DEPTH, D_MODEL, FFN_HIDDEN), D_MODEL),
        "ffn1_w_up": w((DEPTH, D_MODEL, FFN_HIDDEN), D_MODEL),
        "ffn1_w_down": w((DEPTH, FFN_HIDDEN, D_MODEL), FFN_HIDDEN),
        "mix_norm": gain((DEPTH, D_MODEL)),
        "da_w_in": w((N_ATTN_LAYERS, D_MODEL, DA_IN_WIDTH), D_MODEL),
        "da_lambda_q1": small((N_ATTN_LAYERS, DA_HEAD_DIM), 0.1),
        "da_lambda_k1": small((N_ATTN_LAYERS, DA_HEAD_DIM), 0.1),
        "da_lambda_q2": small((N_ATTN_LAYERS, DA_HEAD_DIM), 0.1),
        "da_lambda_k2": small((N_ATTN_LAYERS, DA_HEAD_DIM), 0.1),
        "da_subln": gain((N_ATTN_LAYERS, 2 * DA_HEAD_DIM)),
        "da_w_out": w((N_ATTN_LAYERS, DA_V_WIDTH, D_MODEL), DA_V_WIDTH),
        "ssd_w_in": w((N_SSD_LAYERS, D_MODEL, SSD_IN_WIDTH), D_MODEL),
        "ssd_conv_w": w((N_SSD_LAYERS, SSD_CONV_WIDTH, SSD_CONV_DIM), SSD_CONV_WIDTH),
        "ssd_conv_b": small((N_SSD_LAYERS, SSD_CONV_DIM), 0.01),
        "ssd_dt_bias": ssd_dt_bias,
        "ssd_A_log": ssd_A_log,
        "ssd_D": gain((N_SSD_LAYERS, SSD_N_HEADS)),
        "ssd_gnorm": gain((N_SSD_LAYERS, SSD_D_INNER)),
        "ssd_w_out": w((N_SSD_LAYERS, SSD_D_INNER, D_MODEL), SSD_D_INNER),
        "xa_norm": gain((DEPTH, D_MODEL)),
        "xa_mem_norm": gain((DEPTH, D_MODEL)),
        "xa_w_q": w((DEPTH, D_MODEL, D_MODEL), D_MODEL),
        "xa_w_kv": w((DEPTH, D_MODEL, 2 * D_MODEL), D_MODEL),
        "xa_w_o": w((DEPTH, D_MODEL, D_MODEL), D_MODEL),
        "ffn2_norm": gain((DEPTH, D_MODEL)),
        "ffn2_w_gate": w((DEPTH, D_MODEL, FFN_HIDDEN), D_MODEL),
        "ffn2_w_up": w((DEPTH, D_MODEL, FFN_HIDDEN), D_MODEL),
        "ffn2_w_down": w((DEPTH, FFN_HIDDEN, D_MODEL), FFN_HIDDEN),
        "final_norm": gain((D_MODEL,)),
    }


def reference(x, mem, positions, ffn1_norm, ffn1_w_gate, ffn1_w_up, ffn1_w_down, mix_norm,
              da_w_in, da_lambda_q1, da_lambda_k1, da_lambda_q2, da_lambda_k2, da_subln, da_w_out,
              ssd_w_in, ssd_conv_w, ssd_conv_b, ssd_dt_bias, ssd_A_log, ssd_D, ssd_gnorm, ssd_w_out,
              xa_norm, xa_mem_norm, xa_w_q, xa_w_kv, xa_w_o,
              ffn2_norm, ffn2_w_gate, ffn2_w_up, ffn2_w_down, final_norm):
    cos, sin = rope_tables(positions)
    h = x
    for i in range(DEPTH):
        h = h + 0.5 * swiglu(rmsnorm(h, ffn1_norm[i]), ffn1_w_gate[i], ffn1_w_up[i], ffn1_w_down[i])
        hn = rmsnorm(h, mix_norm[i])
        j = i // N_MIXERS
        if i % N_MIXERS == 0:
            h = h + diff_attention(hn, cos, sin, da_w_in[j], da_lambda_q1[j], da_lambda_k1[j],
                                   da_lambda_q2[j], da_lambda_k2[j], da_subln[j], da_w_out[j],
                                   lambda_init_fn(i))
        else:
            h = h + ssd_mixer(hn, ssd_w_in[j], ssd_conv_w[j], ssd_conv_b[j], ssd_dt_bias[j],
                              ssd_A_log[j], ssd_D[j], ssd_gnorm[j], ssd_w_out[j])
        h = h + memory_cross_attention(rmsnorm(h, xa_norm[i]), mem, xa_mem_norm[i],
                                       xa_w_q[i], xa_w_kv[i], xa_w_o[i])
        h = h + 0.5 * swiglu(rmsnorm(h, ffn2_norm[i]), ffn2_w_gate[i], ffn2_w_up[i], ffn2_w_down[i])
    return rmsnorm(h, final_norm)
```

```python
import functools
import math

import jax
import jax.numpy as jnp
from jax import lax
from jax.experimental import pallas as pl
from jax.experimental.pallas import tpu as pltpu

F32 = jnp.float32
BF16 = jnp.bfloat16

EPS = 1e-6
ROPE_THETA = 10000.0
DA_HEAD_DIM = 64
SSD_HEAD_DIM = 64
SSD_N_GROUPS = 8
SSD_D_STATE = 128
SSD_CONV_WIDTH = 4
SSD_CHUNK = 128
XA_N_HEADS = 4
N_MIXERS = 2

LANES = 128
SUBLANES = 8
MXU_DIM = 256
VMEM_LIMIT = 56 * 1024 * 1024


def _lambda_init(layer_idx):
    return 0.8 - 0.6 * math.exp(-0.3 * layer_idx)


def _params(*semantics):
    return pltpu.CompilerParams(dimension_semantics=semantics, vmem_limit_bytes=VMEM_LIMIT)


def _resident(shape):
    zeros = (0,) * len(shape)
    return pl.BlockSpec(shape, lambda *_: zeros, pipeline_mode=pl.Buffered(1))


def _rmsnorm(x, w):
    return x * lax.rsqrt(jnp.mean(x * x, axis=-1, keepdims=True) + EPS) * w


def _dot(a, b):
    return jnp.dot(a, b, preferred_element_type=F32)


def _dot_nt(a, b):
    return lax.dot_general(a, b, (((1,), (1,)), ((), ())), preferred_element_type=F32)


def _split3(x):
    hi = x.astype(BF16)
    r1 = x - hi.astype(F32)
    mid = r1.astype(BF16)
    lo = (r1 - mid.astype(F32)).astype(BF16)
    return hi, mid, lo


def _ffn_kernel(h_ref, nw_ref, wg_ref, wu_ref, wd_ref, fw_ref, o_ref, *, f_chunks, final):
    h = h_ref[...]
    n = _rmsnorm(h, nw_ref[...]).astype(BF16)
    acc = None
    for c0, c1 in f_chunks:
        g = _dot(n, wg_ref[:, c0:c1])
        u = _dot(n, wu_ref[:, c0:c1])
        a = (g * jax.nn.sigmoid(g) * u).astype(BF16)
        d = _dot(a, wd_ref[c0:c1, :])
        acc = d if acc is None else acc + d
    out = h + 0.5 * acc
    if final:
        out = _rmsnorm(out, fw_ref[...])
    o_ref[...] = out


def _ffn(h, norm_w, wg, wu, wd, final_w, *, final, tm=512, fc=1024):
    T, D = h.shape
    F = wg.shape[1]
    tm = min(tm, T)
    f_chunks = tuple((c, min(c + fc, F)) for c in range(0, F, fc))
    return pl.pallas_call(
        functools.partial(_ffn_kernel, f_chunks=f_chunks, final=final),
        grid=(T // tm,),
        in_specs=[
            pl.BlockSpec((tm, D), lambda i: (i, 0)),
            _resident((1, D)),
            _resident((D, F)),
            _resident((D, F)),
            _resident((F, D)),
            _resident((1, D)),
        ],
        out_specs=pl.BlockSpec((tm, D), lambda i: (i, 0)),
        out_shape=jax.ShapeDtypeStruct((T, D), F32),
        compiler_params=_params("arbitrary"),
        name="ffn_final" if final else "ffn",
    )(h, norm_w, wg, wu, wd, final_w)


def _da_qkv_kernel(h_ref, pos_ref, nw_ref, w_ref, invf_ref, sign_ref, q_ref, k_ref, v_ref, *, width):
    tm = h_ref.shape[0]
    n = _rmsnorm(h_ref[...], nw_ref[...]).astype(BF16)
    pos = jnp.broadcast_to(pos_ref[...], (LANES, tm)).T
    ang = pos * invf_ref[...]
    cos = jnp.cos(ang)
    sin = jnp.sin(ang) * sign_ref[...]
    lane = lax.broadcasted_iota(jnp.int32, (tm, LANES), 1)
    low_half = (lane % DA_HEAD_DIM) < (DA_HEAD_DIM // 2)
    half = DA_HEAD_DIM // 2

    def rope(x):
        partner = jnp.where(low_half, pltpu.roll(x, LANES - half, axis=1), pltpu.roll(x, half, axis=1))
        return x * cos + partner * sin

    q_scale = DA_HEAD_DIM ** -0.5
    q = _dot(n, w_ref[:, :width])
    k = _dot(n, w_ref[:, width:2 * width])
    for c in range(width // LANES):
        sl = slice(c * LANES, (c + 1) * LANES)
        q_ref[:, sl] = (rope(q[:, sl]) * q_scale).astype(BF16)
        k_ref[:, sl] = rope(k[:, sl]).astype(BF16)
    v_ref[...] = _dot(n, w_ref[:, 2 * width:3 * width]).astype(BF16)


def _da_qkv(h, pos, norm_w, w_in, invf, sign, *, tm=512):
    B, S, D = h.shape
    width = w_in.shape[1] // 3
    tm = min(tm, S)
    tok = lambda b, i: (b, i, 0)
    out = jax.ShapeDtypeStruct((B, S, width), BF16)
    return pl.pallas_call(
        functools.partial(_da_qkv_kernel, width=width),
        grid=(B, S // tm),
        in_specs=[
            pl.BlockSpec((None, tm, D), tok),
            pl.BlockSpec((None, 1, tm), lambda b, i: (b, 0, i)),
            _resident((1, D)),
            _resident((D, 3 * width)),
            _resident((1, LANES)),
            _resident((1, LANES)),
        ],
        out_specs=[pl.BlockSpec((None, tm, width), tok)] * 3,
        out_shape=[out, out, out],
        compiler_params=_params("arbitrary", "arbitrary"),
        name="da_qkv",
    )(h, pos, norm_w, w_in, invf, sign)


def _da_attn_kernel(q_ref, k_ref, v_ref, lam_ref, subln_ref, o_ref, *, tq, lambda_init):
    qi = pl.program_id(2)
    q = q_ref[...]
    lane = lax.broadcasted_iota(jnp.int32, (tq, LANES), 1)
    zero = jnp.zeros_like(q)
    qs = jnp.concatenate([jnp.where(lane < DA_HEAD_DIM, q, zero),
                          jnp.where(lane >= DA_HEAD_DIM, q, zero)], axis=0)

    def step(j, carry, diagonal):
        m, l, acc = carry
        start = pl.multiple_of(j * tq, tq)
        k = k_ref[pl.ds(start, tq), :]
        v = v_ref[pl.ds(start, tq), :]
        s = _dot_nt(qs, k)
        if diagonal:
            row = lax.broadcasted_iota(jnp.int32, (2 * tq, tq), 0)
            col = lax.broadcasted_iota(jnp.int32, (2 * tq, tq), 1)
            q_pos = jnp.where(row >= tq, row - tq, row)
            s = jnp.where(col <= q_pos, s, -jnp.inf)
        m_new = jnp.maximum(m, jnp.max(s, axis=-1, keepdims=True))
        alpha = jnp.exp(m - m_new)
        p = jnp.exp(s - m_new)
        l = alpha * l + jnp.sum(p, axis=-1, keepdims=True)
        acc = alpha * acc + _dot(p.astype(BF16), v)
        return m_new, l, acc

    init = (jnp.full((2 * tq, 1), -jnp.inf, F32), jnp.zeros((2 * tq, 1), F32),
            jnp.zeros((2 * tq, LANES), F32))
    carry = lax.fori_loop(0, qi, lambda j, c: step(j, c, False), init)
    _, l, acc = step(qi, carry, True)

    o = acc / l
    lam_v = lam_ref[...]
    lam = (jnp.exp(jnp.sum(lam_v[0:1] * lam_v[1:2], axis=-1, keepdims=True))
           - jnp.exp(jnp.sum(lam_v[2:3] * lam_v[3:4], axis=-1, keepdims=True))
           + lambda_init)
    d = o[:tq] - lam * o[tq:]
    y = _rmsnorm(d, subln_ref[...]) * (1.0 - lambda_init)
    o_ref[...] = y.astype(BF16)


def _da_attn(q, k, v, lam_vecs, subln, *, lambda_init, tq=512):
    B, S, W = q.shape
    H = W // LANES
    tq = min(tq, S)
    return pl.pallas_call(
        functools.partial(_da_attn_kernel, tq=tq, lambda_init=lambda_init),
        grid=(B, H, S // tq),
        in_specs=[
            pl.BlockSpec((None, tq, LANES), lambda b, h, i: (b, i, h)),
            pl.BlockSpec((None, S, LANES), lambda b, h, i: (b, 0, h)),
            pl.BlockSpec((None, S, LANES), lambda b, h, i: (b, 0, h)),
            _resident(lam_vecs.shape),
            _resident((1, LANES)),
        ],
        out_specs=pl.BlockSpec((None, tq, LANES), lambda b, h, i: (b, i, h)),
        out_shape=jax.ShapeDtypeStruct((B, S, W), BF16),
        compiler_params=_params("arbitrary", "arbitrary", "arbitrary"),
        name="da_attn",
    )(q, k, v, lam_vecs, subln)


def _mem_kv_kernel(mem_ref, nw_ref, w_ref, k_ref, v_ref):
    D = mem_ref.shape[-1]
    m = _rmsnorm(mem_ref[...], nw_ref[...]).astype(BF16)
    kv = _dot(m, w_ref[...])
    k_ref[...] = kv[:, :D].astype(BF16)
    v_ref[...] = kv[:, D:].astype(BF16)


def _mem_kv(mem, norm_w, w_kv):
    B, M, D = mem.shape
    L = w_kv.shape[0]
    out = jax.ShapeDtypeStruct((L, B, M, D), BF16)
    return pl.pallas_call(
        _mem_kv_kernel,
        grid=(L, B),
        in_specs=[
            pl.BlockSpec((None, M, D), lambda l, b: (b, 0, 0)),
            pl.BlockSpec((None, 1, D), lambda l, b: (l, 0, 0)),
            pl.BlockSpec((None, D, 2 * D), lambda l, b: (l, 0, 0)),
        ],
        out_specs=[pl.BlockSpec((None, None, M, D), lambda l, b: (l, b, 0, 0))] * 2,
        out_shape=[out, out],
        compiler_params=_params("arbitrary", "arbitrary"),
        name="mem_kv",
    )(mem, norm_w, w_kv)


def _xattn_kernel(h_ref, a_ref, wpre_ref, nw_ref, wq_ref, k_ref, v_ref, wo_ref, o_ref, *, n_heads):
    h = h_ref[...] + _dot(a_ref[...], wpre_ref[...])
    D = h.shape[-1]
    hd = D // n_heads
    n = _rmsnorm(h, nw_ref[...]).astype(BF16)
    q = (_dot(n, wq_ref[...]) * (hd ** -0.5)).astype(BF16)
    heads = []
    for i in range(n_heads):
        sl = slice(i * hd, (i + 1) * hd)
        s = _dot_nt(q[:, sl], k_ref[:, sl])
        p = jnp.exp(s - jnp.max(s, axis=-1, keepdims=True))
        l = jnp.sum(p, axis=-1, keepdims=True)
        heads.append((_dot(p.astype(BF16), v_ref[:, sl]) / l).astype(BF16))
    o = jnp.concatenate(heads, axis=-1)
    o_ref[...] = h + _dot(o, wo_ref[...])


def _xattn(h, a, w_pre, norm_w, wq, k, v, wo, *, tm=512):
    B, S, D = h.shape
    Ka = a.shape[-1]
    M = k.shape[1]
    tm = min(tm, S)
    tok = lambda b, i: (b, i, 0)
    return pl.pallas_call(
        functools.partial(_xattn_kernel, n_heads=XA_N_HEADS),
        grid=(B, S // tm),
        in_specs=[
            pl.BlockSpec((None, tm, D), tok),
            pl.BlockSpec((None, tm, Ka), tok),
            _resident((Ka, D)),
            _resident((1, D)),
            _resident((D, D)),
            pl.BlockSpec((None, M, D), lambda b, i: (b, 0, 0)),
            pl.BlockSpec((None, M, D), lambda b, i: (b, 0, 0)),
            _resident((D, D)),
        ],
        out_specs=pl.BlockSpec((None, tm, D), tok),
        out_shape=jax.ShapeDtypeStruct((B, S, D), F32),
        compiler_params=_params("arbitrary", "arbitrary"),
        name="xattn",
    )(h, a, w_pre, norm_w, wq, k, v, wo)


def _softplus(x):
    return jnp.maximum(x, 0.0) + jnp.log(1.0 + jnp.exp(-jnp.abs(x)))


def _ssd_in_kernel(h_ref, nw_ref, wz_ref, wx_ref, wdt_ref, cw_ref, cb_ref, dtb_ref,
                   z_ref, xs_ref, b_ref, c_ref, dt_ref, tail_ref, *, d_inner, bc_width, col_chunk):
    tm = h_ref.shape[0]

    @pl.when(pl.program_id(1) == 0)
    def _():
        tail_ref[...] = jnp.zeros_like(tail_ref)

    n = _rmsnorm(h_ref[...], nw_ref[...]).astype(BF16)
    z_ref[...] = _dot(n, wz_ref[...])
    dt_ref[...] = _softplus(_dot(n, wdt_ref[...]) + dtb_ref[...])

    row8 = lax.broadcasted_iota(jnp.int32, (SUBLANES, col_chunk), 0)
    conv_dim = wx_ref.shape[1]
    for c0 in range(0, conv_dim, col_chunk):
        cs = slice(c0, c0 + col_chunk)
        u = _dot(n, wx_ref[:, cs])
        tail = tail_ref[:, cs]
        w = cw_ref[:, cs]
        acc = u * w[SSD_CONV_WIDTH - 1:SSD_CONV_WIDTH] + cb_ref[:, cs]
        for k in range(1, SSD_CONV_WIDTH):
            rolled = pltpu.roll(u, k, axis=0)
            first = jnp.where(row8 < k, pltpu.roll(tail, k, axis=0), rolled[:SUBLANES])
            shifted = jnp.concatenate([first, rolled[SUBLANES:]], axis=0)
            acc = acc + shifted * w[SSD_CONV_WIDTH - 1 - k:SSD_CONV_WIDTH - k]
        tail_ref[:, cs] = u[tm - SUBLANES:]
        y = acc * jax.nn.sigmoid(acc)
        if c0 < d_inner:
            xs_ref[:, cs] = y
        elif c0 < d_inner + bc_width:
            b_ref[:, c0 - d_inner:c0 - d_inner + col_chunk] = y.astype(BF16)
        else:
            c_ref[:, c0 - d_inner - bc_width:c0 - d_inner - bc_width + col_chunk] = y.astype(BF16)


def _ssd_in(h, norm_w, wz, wx, wdt, conv_w, conv_b, dt_bias, *, tm=256, col_chunk=512):
    B, S, D = h.shape
    d_inner = wz.shape[1]
    conv_dim = wx.shape[1]
    bc_width = (conv_dim - d_inner) // 2
    tm = min(tm, S)
    tok = lambda b, i: (b, i, 0)
    return pl.pallas_call(
        functools.partial(_ssd_in_kernel, d_inner=d_inner, bc_width=bc_width, col_chunk=col_chunk),
        grid=(B, S // tm),
        in_specs=[
            pl.BlockSpec((None, tm, D), tok),
            _resident((1, D)),
            _resident((D, d_inner)),
            _resident((D, conv_dim)),
            _resident((D, LANES)),
            _resident((SSD_CONV_WIDTH, conv_dim)),
            _resident((1, conv_dim)),
            _resident((1, LANES)),
        ],
        out_specs=[
            pl.BlockSpec((None, tm, d_inner), tok),
            pl.BlockSpec((None, tm, d_inner), tok),
            pl.BlockSpec((None, tm, bc_width), tok),
            pl.BlockSpec((None, tm, bc_width), tok),
            pl.BlockSpec((None, tm, LANES), tok),
        ],
        out_shape=[
            jax.ShapeDtypeStruct((B, S, d_inner), F32),
            jax.ShapeDtypeStruct((B, S, d_inner), F32),
            jax.ShapeDtypeStruct((B, S, bc_width), BF16),
            jax.ShapeDtypeStruct((B, S, bc_width), BF16),
            jax.ShapeDtypeStruct((B, S, LANES), F32),
        ],
        scratch_shapes=[pltpu.VMEM((SUBLANES, conv_dim), F32)],
        compiler_params=_params("arbitrary", "arbitrary"),
        name="ssd_in",
    )(h, norm_w, wz, wx, wdt, conv_w, conv_b, dt_bias)


def _ssd_core_kernel(xs_ref, b_ref, c_ref, dt_ref, z_ref, alog_ref, dskip_ref, gn_ref, expand_ref,
                     y_ref, state_ref, *, n_groups, heads_per_group):
    L = xs_ref.shape[0]
    gw = heads_per_group * SSD_HEAD_DIM

    @pl.when(pl.program_id(1) == 0)
    def _():
        state_ref[...] = jnp.zeros_like(state_ref)

    row = lax.broadcasted_iota(jnp.int32, (L, L), 0)
    col = lax.broadcasted_iota(jnp.int32, (L, L), 1)
    causal = row >= col
    tril = causal.astype(BF16)
    expand = expand_ref[...]

    dt = dt_ref[...]
    a = dt * -jnp.exp(alog_ref[...])
    a_cs = sum(_dot(tril, part) for part in _split3(a))
    a_cs_t = a_cs.T
    dt_full = sum(_dot(part, expand) for part in _split3(dt))
    acs_full = sum(_dot(part, expand) for part in _split3(a_cs))
    a_end = acs_full[L - 1:L, :]

    x = xs_ref[...]
    x_dt = x * dt_full
    x_end = (x_dt * jnp.exp(a_end - acs_full)).astype(BF16)
    carry_in = jnp.exp(acs_full)
    chunk_decay = jnp.exp(a_end)
    head_of_lane = lax.broadcasted_iota(jnp.int32, (L, gw), 1) // SSD_HEAD_DIM

    for g in range(n_groups):
        gs = slice(g * gw, (g + 1) * gw)
        ns = slice(g * SSD_D_STATE, (g + 1) * SSD_D_STATE)
        b_g = b_ref[:, ns]
        c_g = c_ref[:, ns]
        cb = _dot_nt(c_g, b_g)
        state = state_ref[:, gs]
        y = _dot(c_g, state.astype(BF16)) * carry_in[:, gs]
        x_dt_g = x_dt[:, gs]
        for r in range(heads_per_group):
            hh = g * heads_per_group + r
            seg = a_cs[:, hh:hh + 1] - a_cs_t[hh:hh + 1, :]
            decay = jnp.exp(jnp.where(causal, seg, -jnp.inf))
            m = (cb * decay).astype(BF16)
            x_r = jnp.where(head_of_lane == r, x_dt_g, 0.0).astype(BF16)
            y = y + _dot(m, x_r)
        b_t = b_g.astype(F32).T.astype(BF16)
        state_ref[:, gs] = state * chunk_decay[:, gs] + _dot(b_t, x_end[:, gs])

        y = y + dskip_ref[:, gs] * x[:, gs]
        zg = z_ref[:, gs]
        y = y * (zg * jax.nn.sigmoid(zg))
        y = y * lax.rsqrt(jnp.mean(y * y, axis=-1, keepdims=True) + EPS) * gn_ref[:, gs]
        y_ref[:, gs] = y.astype(BF16)


def _ssd_core(xs, bm, cm, dt, z, a_log, d_skip, gnorm, expand):
    B, S, d_inner = xs.shape
    n_groups = SSD_N_GROUPS
    heads_per_group = d_inner // SSD_HEAD_DIM // n_groups
    bc_width = bm.shape[-1]
    L = SSD_CHUNK
    tok = lambda b, i: (b, i, 0)
    return pl.pallas_call(
        functools.partial(_ssd_core_kernel, n_groups=n_groups, heads_per_group=heads_per_group),
        grid=(B, S // L),
        in_specs=[
            pl.BlockSpec((None, L, d_inner), tok),
            pl.BlockSpec((None, L, bc_width), tok),
            pl.BlockSpec((None, L, bc_width), tok),
            pl.BlockSpec((None, L, LANES), tok),
            pl.BlockSpec((None, L, d_inner), tok),
            _resident((1, LANES)),
            _resident((1, d_inner)),
            _resident((1, d_inner)),
            _resident((LANES, d_inner)),
        ],
        out_specs=pl.BlockSpec((None, L, d_inner), tok),
        out_shape=jax.ShapeDtypeStruct((B, S, d_inner), BF16),
        scratch_shapes=[pltpu.VMEM((SSD_D_STATE, d_inner), F32)],
        compiler_params=_params("arbitrary", "arbitrary"),
        name="ssd_core",
    )(xs, bm, cm, dt, z, a_log, d_skip, gnorm, expand)


def kernel(x, mem, positions, ffn1_norm, ffn1_w_gate, ffn1_w_up, ffn1_w_down, mix_norm, da_w_in, da_lambda_q1, da_lambda_k1, da_lambda_q2, da_lambda_k2, da_subln, da_w_out, ssd_w_in, ssd_conv_w, ssd_conv_b, ssd_dt_bias, ssd_A_log, ssd_D, ssd_gnorm, ssd_w_out, xa_norm, xa_mem_norm, xa_w_q, xa_w_kv, xa_w_o, ffn2_norm, ffn2_w_gate, ffn2_w_up, ffn2_w_down, final_norm):
    B, S, D = x.shape
    depth = ffn1_norm.shape[0]
    T = B * S
    row = lambda v: v.reshape(1, -1).astype(F32)
    bf = lambda w: w.astype(BF16)

    inv = 1.0 / (ROPE_THETA ** (jnp.arange(0, DA_HEAD_DIM, 2, dtype=F32) / DA_HEAD_DIM))
    invf = jnp.tile(inv, LANES // inv.shape[0]).reshape(1, LANES)
    lane = jnp.arange(LANES)
    sign = jnp.where((lane % DA_HEAD_DIM) < DA_HEAD_DIM // 2, -1.0, 1.0).astype(F32).reshape(1, LANES)
    pos = positions.astype(F32).reshape(B, 1, S)

    mem_k, mem_v = _mem_kv(mem, xa_mem_norm.reshape(depth, 1, D), bf(xa_w_kv))

    h = x
    for i in range(depth):
        h = _ffn(h.reshape(T, D), row(ffn1_norm[i]), bf(ffn1_w_gate[i]), bf(ffn1_w_up[i]),
                 bf(ffn1_w_down[i]), row(final_norm), final=False).reshape(B, S, D)
        j = i // N_MIXERS
        if i % N_MIXERS == 0:
            q, k, v = _da_qkv(h, pos, row(mix_norm[i]), bf(da_w_in[j]), invf, sign)
            lam_vecs = jnp.stack([da_lambda_q1[j], da_lambda_k1[j], da_lambda_q2[j], da_lambda_k2[j]]).astype(F32)
            mixed = _da_attn(q, k, v, lam_vecs, row(da_subln[j]), lambda_init=_lambda_init(i))
            w_mix_out = bf(da_w_out[j])
        else:
            d_inner = ssd_w_out.shape[1]
            n_heads = ssd_A_log.shape[1]
            conv_dim = ssd_conv_w.shape[2]
            w_in = ssd_w_in[j]
            wz = bf(w_in[:, :d_inner])
            wx = bf(w_in[:, d_inner:d_inner + conv_dim])
            wdt = bf(jnp.pad(w_in[:, d_inner + conv_dim:], ((0, 0), (0, LANES - n_heads))))
            dt_bias = jnp.pad(ssd_dt_bias[j].astype(F32), (0, LANES - n_heads)).reshape(1, LANES)
            z, xs, bm, cm, dt = _ssd_in(h, row(mix_norm[i]), wz, wx, wdt, ssd_conv_w[j].astype(F32),
                                        row(ssd_conv_b[j]), dt_bias)
            a_log = jnp.pad(ssd_A_log[j].astype(F32), (0, LANES - n_heads)).reshape(1, LANES)
            d_skip = jnp.repeat(ssd_D[j].astype(F32), SSD_HEAD_DIM).reshape(1, d_inner)
            expand = (jnp.arange(LANES)[:, None] == (jnp.arange(d_inner) // SSD_HEAD_DIM)[None, :]).astype(BF16)
            mixed = _ssd_core(xs, bm, cm, dt, z, a_log, d_skip, row(ssd_gnorm[j]), expand)
            w_mix_out = bf(ssd_w_out[j])
        h = _xattn(h, mixed, w_mix_out, row(xa_norm[i]), bf(xa_w_q[i]), mem_k[i], mem_v[i], bf(xa_w_o[i]))
        h = _ffn(h.reshape(T, D), row(ffn2_norm[i]), bf(ffn2_w_gate[i]), bf(ffn2_w_up[i]),
                 bf(ffn2_w_down[i]), row(final_norm), final=(i == depth - 1)).reshape(B, S, D)
    return h
```

```python
import functools
import math

import jax
import jax.numpy as jnp
from jax import lax
from jax.experimental import pallas as pl
from jax.experimental.pallas import tpu as pltpu

F32 = jnp.float32
BF16 = jnp.bfloat16

EPS = 1e-6
ROPE_THETA = 10000.0
DA_HEAD_DIM = 64
SSD_HEAD_DIM = 64
SSD_N_GROUPS = 8
SSD_D_STATE = 128
SSD_CONV_WIDTH = 4
SSD_CHUNK = 128
XA_N_HEADS = 4
N_MIXERS = 2

LANES = 128
SUBLANES = 8
BF16_ROWS = 16
MXU_DIM = 256
VMEM_LIMIT = 56 * 1024 * 1024


def _lambda_init(layer_idx):
    return 0.8 - 0.6 * math.exp(-0.3 * layer_idx)


def _params(*semantics):
    return pltpu.CompilerParams(dimension_semantics=semantics, vmem_limit_bytes=VMEM_LIMIT)


def _resident(shape):
    zeros = (0,) * len(shape)
    return pl.BlockSpec(shape, lambda *_: zeros, pipeline_mode=pl.Buffered(1))


def _rmsnorm(x, w):
    return x * lax.rsqrt(jnp.mean(x * x, axis=-1, keepdims=True) + EPS) * w


def _dot(a, b):
    return jnp.dot(a, b, preferred_element_type=F32)


def _dot_nt(a, b):
    return lax.dot_general(a, b, (((1,), (1,)), ((), ())), preferred_element_type=F32)


def _split3(x):
    hi = x.astype(BF16)
    r1 = x - hi.astype(F32)
    mid = r1.astype(BF16)
    lo = (r1 - mid.astype(F32)).astype(BF16)
    return hi, mid, lo


def _ffn_kernel(h_ref, nw_ref, wg_ref, wu_ref, wd_ref, fw_ref, o_ref, *, f_chunks, final):
    h = h_ref[...]
    n = _rmsnorm(h, nw_ref[...]).astype(BF16)
    acc = None
    for c0, c1 in f_chunks:
        g = _dot(n, wg_ref[:, c0:c1])
        u = _dot(n, wu_ref[:, c0:c1])
        a = (g * jax.nn.sigmoid(g) * u).astype(BF16)
        d = _dot(a, wd_ref[c0:c1, :])
        acc = d if acc is None else acc + d
    out = h + 0.5 * acc
    if final:
        out = _rmsnorm(out, fw_ref[...])
    o_ref[...] = out


def _ffn(h, norm_w, wg, wu, wd, final_w, *, final, tm=512, fc=1024):
    T, D = h.shape
    F = wg.shape[1]
    tm = min(tm, T)
    f_chunks = tuple((c, min(c + fc, F)) for c in range(0, F, fc))
    return pl.pallas_call(
        functools.partial(_ffn_kernel, f_chunks=f_chunks, final=final),
        grid=(T // tm,),
        in_specs=[
            pl.BlockSpec((tm, D), lambda i: (i, 0)),
            _resident((1, D)),
            _resident((D, F)),
            _resident((D, F)),
            _resident((F, D)),
            _resident((1, D)),
        ],
        out_specs=pl.BlockSpec((tm, D), lambda i: (i, 0)),
        out_shape=jax.ShapeDtypeStruct((T, D), F32),
        compiler_params=_params("arbitrary"),
        name="ffn_final" if final else "ffn",
    )(h, norm_w, wg, wu, wd, final_w)


def _da_qkv_kernel(h_ref, pos_ref, nw_ref, w_ref, invf_ref, sign_ref, q_ref, k_ref, v_ref, *, width):
    tm = h_ref.shape[0]
    n = _rmsnorm(h_ref[...], nw_ref[...]).astype(BF16)
    pos = jnp.broadcast_to(pos_ref[...], (LANES, tm)).T
    ang = pos * invf_ref[...]
    cos = jnp.cos(ang)
    sin = jnp.sin(ang) * sign_ref[...]
    lane = lax.broadcasted_iota(jnp.int32, (tm, LANES), 1)
    low_half = (lane % DA_HEAD_DIM) < (DA_HEAD_DIM // 2)
    half = DA_HEAD_DIM // 2

    def rope(x):
        partner = jnp.where(low_half, pltpu.roll(x, LANES - half, axis=1), pltpu.roll(x, half, axis=1))
        return x * cos + partner * sin

    q_scale = DA_HEAD_DIM ** -0.5 * math.log2(math.e)
    q = _dot(n, w_ref[:, :width])
    k = _dot(n, w_ref[:, width:2 * width])
    for c in range(width // LANES):
        sl = slice(c * LANES, (c + 1) * LANES)
        q_ref[:, sl] = (rope(q[:, sl]) * q_scale).astype(BF16)
        k_ref[:, sl] = rope(k[:, sl]).astype(BF16)
    v_ref[...] = _dot(n, w_ref[:, 2 * width:3 * width]).astype(BF16)


def _da_qkv(h, pos, norm_w, w_in, invf, sign, *, tm=512):
    B, S, D = h.shape
    width = w_in.shape[1] // 3
    tm = min(tm, S)
    tok = lambda b, i: (b, i, 0)
    out = jax.ShapeDtypeStruct((B, S, width), BF16)
    return pl.pallas_call(
        functools.partial(_da_qkv_kernel, width=width),
        grid=(B, S // tm),
        in_specs=[
            pl.BlockSpec((None, tm, D), tok),
            pl.BlockSpec((None, 1, tm), lambda b, i: (b, 0, i)),
            _resident((1, D)),
            _resident((D, 3 * width)),
            _resident((1, LANES)),
            _resident((1, LANES)),
        ],
        out_specs=[pl.BlockSpec((None, tm, width), tok)] * 3,
        out_shape=[out, out, out],
        compiler_params=_params("arbitrary", "arbitrary"),
        name="da_qkv",
    )(h, pos, norm_w, w_in, invf, sign)


def _transpose_bf16(x):
    return x.astype(F32).T.astype(BF16)


def _da_attn_kernel(q_ref, k_ref, v_ref, lam_ref, subln_ref, o_ref, *scratch, tq, heads, lambda_init):
    qi = pl.program_id(2)
    S = k_ref.shape[0]
    head_lanes = [slice(hh * LANES, (hh + 1) * LANES) for hh in range(heads)]
    vt_refs, s_refs, p_refs, acc_refs, st_refs = (scratch[i * heads:(i + 1) * heads] for i in range(5))

    @pl.when(qi == 0)
    def _():
        for hh, hl in enumerate(head_lanes):
            vt_refs[hh][LANES:, :] = jnp.ones((BF16_ROWS, S), BF16)
            for c in range(0, S, tq):
                vt_refs[hh][:LANES, c:c + tq] = _transpose_bf16(v_ref[c:c + tq, hl])

    sub = lax.broadcasted_iota(jnp.int32, (LANES, tq), 0)
    qst = []
    for hh, hl in enumerate(head_lanes):
        qt = _transpose_bf16(q_ref[:, hl])
        zero = jnp.zeros_like(qt)
        qst.append(jnp.concatenate([jnp.where(sub < DA_HEAD_DIM, qt, zero),
                                    jnp.where(sub >= DA_HEAD_DIM, qt, zero)], axis=1))
        acc_refs[hh][...] = jnp.zeros_like(acc_refs[hh])
        st_refs[hh][0:1, :] = jnp.full((1, 2 * tq), -jnp.inf, F32)

    key = lax.broadcasted_iota(jnp.int32, (tq, LANES), 0)
    lane = lax.broadcasted_iota(jnp.int32, (tq, LANES), 1)

    def scores(hh, j):
        start = pl.multiple_of(j * tq, tq)
        s_refs[hh][...] = _dot(k_ref[pl.ds(start, tq), head_lanes[hh]], qst[hh])

    def softmax(hh, diagonal):
        s_ref, p_ref, st_ref = s_refs[hh], p_refs[hh], st_refs[hh]
        for c in range(0, 2 * tq, LANES):
            cs = slice(c, c + LANES)

            def strip():
                s = s_ref[:, cs]
                return jnp.where(key <= lane + (c % tq), s, -jnp.inf) if diagonal else s

            m_old = st_ref[0:1, cs]
            m_new = jnp.maximum(m_old, jnp.max(strip(), axis=0, keepdims=True))
            st_ref[0:1, cs] = m_new
            st_ref[1:2, cs] = jnp.exp2(m_old - m_new)
            p_ref[:, cs] = jnp.exp2(strip() - m_new).astype(BF16)

    def weighted_values(hh, j):
        start = pl.multiple_of(j * tq, tq)
        pv = _dot(vt_refs[hh][:, pl.ds(start, tq)], p_refs[hh][...])
        acc_refs[hh][...] = st_refs[hh][1:2, :] * acc_refs[hh][...] + pv

    def step(j, diagonal):
        for hh in range(heads):
            if hh + 1 < heads:
                scores(hh + 1, j)
            elif not diagonal:
                scores(0, j + 1)
            softmax(hh, diagonal)
            weighted_values(hh, j)

    def body(j, carry):
        step(j, False)
        return carry

    scores(0, 0)
    lax.fori_loop(0, qi, body, 0)
    step(qi, True)

    lam_v = lam_ref[...]
    lam = (jnp.exp(jnp.sum(lam_v[0:1] * lam_v[1:2], axis=-1, keepdims=True))
           - jnp.exp(jnp.sum(lam_v[2:3] * lam_v[3:4], axis=-1, keepdims=True))
           + lambda_init)
    for hh, hl in enumerate(head_lanes):
        o = acc_refs[hh][:LANES, :] / acc_refs[hh][LANES:LANES + 1, :]
        d = o[:, :tq] - lam * o[:, tq:]
        y = d * lax.rsqrt(jnp.mean(d * d, axis=0, keepdims=True) + EPS) * subln_ref[...]
        o_ref[:, hl] = (y * (1.0 - lambda_init)).T.astype(BF16)


def _da_attn(q, k, v, lam_vecs, subln_col, *, lambda_init, tq=512, heads=2):
    B, S, W = q.shape
    H = W // LANES
    tq = min(tq, S)
    hw = heads * LANES
    per_head = lambda shape, dtype: [pltpu.VMEM(shape, dtype) for _ in range(heads)]
    return pl.pallas_call(
        functools.partial(_da_attn_kernel, tq=tq, heads=heads, lambda_init=lambda_init),
        grid=(B, H // heads, S // tq),
        in_specs=[
            pl.BlockSpec((None, tq, hw), lambda b, h, i: (b, i, h)),
            pl.BlockSpec((None, S, hw), lambda b, h, i: (b, 0, h), pipeline_mode=pl.Buffered(1)),
            pl.BlockSpec((None, S, hw), lambda b, h, i: (b, 0, h), pipeline_mode=pl.Buffered(1)),
            _resident(lam_vecs.shape),
            _resident((LANES, 1)),
        ],
        out_specs=pl.BlockSpec((None, tq, hw), lambda b, h, i: (b, i, h)),
        out_shape=jax.ShapeDtypeStruct((B, S, W), BF16),
        scratch_shapes=(per_head((LANES + BF16_ROWS, S), BF16) + per_head((tq, 2 * tq), F32)
                        + per_head((tq, 2 * tq), BF16) + per_head((LANES + BF16_ROWS, 2 * tq), F32)
                        + per_head((SUBLANES, 2 * tq), F32)),
        compiler_params=_params("arbitrary", "arbitrary", "arbitrary"),
        name="da_attn",
    )(q, k, v, lam_vecs, subln_col)


def _mem_kv_kernel(mem_ref, nw_ref, w_ref, k_ref, v_ref):
    D = mem_ref.shape[-1]
    m = _rmsnorm(mem_ref[...], nw_ref[...]).astype(BF16)
    kv = _dot(m, w_ref[...])
    k_ref[...] = kv[:, :D].astype(BF16)
    v_ref[...] = kv[:, D:].astype(BF16)


def _mem_kv(mem, norm_w, w_kv):
    B, M, D = mem.shape
    L = w_kv.shape[0]
    out = jax.ShapeDtypeStruct((L, B, M, D), BF16)
    return pl.pallas_call(
        _mem_kv_kernel,
        grid=(L, B),
        in_specs=[
            pl.BlockSpec((None, M, D), lambda l, b: (b, 0, 0)),
            pl.BlockSpec((None, 1, D), lambda l, b: (l, 0, 0)),
            pl.BlockSpec((None, D, 2 * D), lambda l, b: (l, 0, 0)),
        ],
        out_specs=[pl.BlockSpec((None, None, M, D), lambda l, b: (l, b, 0, 0))] * 2,
        out_shape=[out, out],
        compiler_params=_params("arbitrary", "arbitrary"),
        name="mem_kv",
    )(mem, norm_w, w_kv)


def _xattn_kernel(h_ref, a_ref, wpre_ref, nw_ref, wq_ref, k_ref, v_ref, wo_ref, o_ref, *, n_heads):
    h = h_ref[...] + _dot(a_ref[...], wpre_ref[...])
    D = h.shape[-1]
    hd = D // n_heads
    n = _rmsnorm(h, nw_ref[...]).astype(BF16)
    q = (_dot(n, wq_ref[...]) * (hd ** -0.5)).astype(BF16)
    heads = []
    for i in range(n_heads):
        sl = slice(i * hd, (i + 1) * hd)
        s = _dot_nt(q[:, sl], k_ref[:, sl])
        p = jnp.exp(s - jnp.max(s, axis=-1, keepdims=True))
        l = jnp.sum(p, axis=-1, keepdims=True)
        heads.append((_dot(p.astype(BF16), v_ref[:, sl]) / l).astype(BF16))
    o = jnp.concatenate(heads, axis=-1)
    o_ref[...] = h + _dot(o, wo_ref[...])


def _xattn(h, a, w_pre, norm_w, wq, k, v, wo, *, tm=512):
    B, S, D = h.shape
    Ka = a.shape[-1]
    M = k.shape[1]
    tm = min(tm, S)
    tok = lambda b, i: (b, i, 0)
    return pl.pallas_call(
        functools.partial(_xattn_kernel, n_heads=XA_N_HEADS),
        grid=(B, S // tm),
        in_specs=[
            pl.BlockSpec((None, tm, D), tok),
            pl.BlockSpec((None, tm, Ka), tok),
            _resident((Ka, D)),
            _resident((1, D)),
            _resident((D, D)),
            pl.BlockSpec((None, M, D), lambda b, i: (b, 0, 0)),
            pl.BlockSpec((None, M, D), lambda b, i: (b, 0, 0)),
            _resident((D, D)),
        ],
        out_specs=pl.BlockSpec((None, tm, D), tok),
        out_shape=jax.ShapeDtypeStruct((B, S, D), F32),
        compiler_params=_params("arbitrary", "arbitrary"),
        name="xattn",
    )(h, a, w_pre, norm_w, wq, k, v, wo)


def _softplus(x):
    return jnp.maximum(x, 0.0) + jnp.log(1.0 + jnp.exp(-jnp.abs(x)))


def _ssd_in_kernel(h_ref, nw_ref, wz_ref, wx_ref, wdt_ref, cw_ref, cb_ref, dtb_ref,
                   z_ref, xs_ref, b_ref, c_ref, dt_ref, tail_ref, *, d_inner, bc_width, col_chunk):
    tm = h_ref.shape[0]

    @pl.when(pl.program_id(1) == 0)
    def _():
        tail_ref[...] = jnp.zeros_like(tail_ref)

    n = _rmsnorm(h_ref[...], nw_ref[...]).astype(BF16)
    z_ref[...] = _dot(n, wz_ref[...])
    dt_ref[...] = _softplus(_dot(n, wdt_ref[...]) + dtb_ref[...])

    row8 = lax.broadcasted_iota(jnp.int32, (SUBLANES, col_chunk), 0)
    conv_dim = wx_ref.shape[1]
    for c0 in range(0, conv_dim, col_chunk):
        cs = slice(c0, c0 + col_chunk)
        u = _dot(n, wx_ref[:, cs])
        tail = tail_ref[:, cs]
        w = cw_ref[:, cs]
        acc = u * w[SSD_CONV_WIDTH - 1:SSD_CONV_WIDTH] + cb_ref[:, cs]
        for k in range(1, SSD_CONV_WIDTH):
            rolled = pltpu.roll(u, k, axis=0)
            first = jnp.where(row8 < k, pltpu.roll(tail, k, axis=0), rolled[:SUBLANES])
            shifted = jnp.concatenate([first, rolled[SUBLANES:]], axis=0)
            acc = acc + shifted * w[SSD_CONV_WIDTH - 1 - k:SSD_CONV_WIDTH - k]
        tail_ref[:, cs] = u[tm - SUBLANES:]
        y = acc * jax.nn.sigmoid(acc)
        if c0 < d_inner:
            xs_ref[:, cs] = y
        elif c0 < d_inner + bc_width:
            b_ref[:, c0 - d_inner:c0 - d_inner + col_chunk] = y.astype(BF16)
        else:
            c_ref[:, c0 - d_inner - bc_width:c0 - d_inner - bc_width + col_chunk] = y.astype(BF16)


def _ssd_in(h, norm_w, wz, wx, wdt, conv_w, conv_b, dt_bias, *, tm=256, col_chunk=512):
    B, S, D = h.shape
    d_inner = wz.shape[1]
    conv_dim = wx.shape[1]
    bc_width = (conv_dim - d_inner) // 2
    tm = min(tm, S)
    tok = lambda b, i: (b, i, 0)
    return pl.pallas_call(
        functools.partial(_ssd_in_kernel, d_inner=d_inner, bc_width=bc_width, col_chunk=col_chunk),
        grid=(B, S // tm),
        in_specs=[
            pl.BlockSpec((None, tm, D), tok),
            _resident((1, D)),
            _resident((D, d_inner)),
            _resident((D, conv_dim)),
            _resident((D, LANES)),
            _resident((SSD_CONV_WIDTH, conv_dim)),
            _resident((1, conv_dim)),
            _resident((1, LANES)),
        ],
        out_specs=[
            pl.BlockSpec((None, tm, d_inner), tok),
            pl.BlockSpec((None, tm, d_inner), tok),
            pl.BlockSpec((None, tm, bc_width), tok),
            pl.BlockSpec((None, tm, bc_width), tok),
            pl.BlockSpec((None, tm, LANES), tok),
        ],
        out_shape=[
            jax.ShapeDtypeStruct((B, S, d_inner), F32),
            jax.ShapeDtypeStruct((B, S, d_inner), F32),
            jax.ShapeDtypeStruct((B, S, bc_width), BF16),
            jax.ShapeDtypeStruct((B, S, bc_width), BF16),
            jax.ShapeDtypeStruct((B, S, LANES), F32),
        ],
        scratch_shapes=[pltpu.VMEM((SUBLANES, conv_dim), F32)],
        compiler_params=_params("arbitrary", "arbitrary"),
        name="ssd_in",
    )(h, norm_w, wz, wx, wdt, conv_w, conv_b, dt_bias)


def _ssd_core_kernel(xs_ref, b_ref, c_ref, dt_ref, z_ref, alog_ref, dskip_ref, gn_ref, expand_ref,
                     y_ref, state_ref, *, n_groups, heads_per_group):
    L = xs_ref.shape[0]
    gw = heads_per_group * SSD_HEAD_DIM

    @pl.when(pl.program_id(1) == 0)
    def _():
        state_ref[...] = jnp.zeros_like(state_ref)

    row = lax.broadcasted_iota(jnp.int32, (L, L), 0)
    col = lax.broadcasted_iota(jnp.int32, (L, L), 1)
    causal = row >= col
    tril = causal.astype(BF16)
    expand = expand_ref[...]

    dt = dt_ref[...]
    a = dt * -jnp.exp(alog_ref[...])
    a_cs = sum(_dot(tril, part) for part in _split3(a))
    a_cs_t = a_cs.T
    dt_full = sum(_dot(part, expand) for part in _split3(dt))
    acs_full = sum(_dot(part, expand) for part in _split3(a_cs))
    a_end = acs_full[L - 1:L, :]

    x = xs_ref[...]
    x_dt = x * dt_full
    x_end = (x_dt * jnp.exp(a_end - acs_full)).astype(BF16)
    carry_in = jnp.exp(acs_full)
    chunk_decay = jnp.exp(a_end)
    head_of_lane = lax.broadcasted_iota(jnp.int32, (L, gw), 1) // SSD_HEAD_DIM

    for g in range(n_groups):
        gs = slice(g * gw, (g + 1) * gw)
        ns = slice(g * SSD_D_STATE, (g + 1) * SSD_D_STATE)
        b_g = b_ref[:, ns]
        c_g = c_ref[:, ns]
        cb = _dot_nt(c_g, b_g)
        state = state_ref[:, gs]
        y = _dot(c_g, state.astype(BF16)) * carry_in[:, gs]
        x_dt_g = x_dt[:, gs]
        for r in range(heads_per_group):
            hh = g * heads_per_group + r
            seg = a_cs[:, hh:hh + 1] - a_cs_t[hh:hh + 1, :]
            decay = jnp.exp(jnp.where(causal, seg, -jnp.inf))
            m = (cb * decay).astype(BF16)
            x_r = jnp.where(head_of_lane == r, x_dt_g, 0.0).astype(BF16)
            y = y + _dot(m, x_r)
        b_t = b_g.astype(F32).T.astype(BF16)
        state_ref[:, gs] = state * chunk_decay[:, gs] + _dot(b_t, x_end[:, gs])

        y = y + dskip_ref[:, gs] * x[:, gs]
        zg = z_ref[:, gs]
        y = y * (zg * jax.nn.sigmoid(zg))
        y = y * lax.rsqrt(jnp.mean(y * y, axis=-1, keepdims=True) + EPS) * gn_ref[:, gs]
        y_ref[:, gs] = y.astype(BF16)


def _ssd_core(xs, bm, cm, dt, z, a_log, d_skip, gnorm, expand):
    B, S, d_inner = xs.shape
    n_groups = SSD_N_GROUPS
    heads_per_group = d_inner // SSD_HEAD_DIM // n_groups
    bc_width = bm.shape[-1]
    L = SSD_CHUNK
    tok = lambda b, i: (b, i, 0)
    return pl.pallas_call(
        functools.partial(_ssd_core_kernel, n_groups=n_groups, heads_per_group=heads_per_group),
        grid=(B, S // L),
        in_specs=[
            pl.BlockSpec((None, L, d_inner), tok),
            pl.BlockSpec((None, L, bc_width), tok),
            pl.BlockSpec((None, L, bc_width), tok),
            pl.BlockSpec((None, L, LANES), tok),
            pl.BlockSpec((None, L, d_inner), tok),
            _resident((1, LANES)),
            _resident((1, d_inner)),
            _resident((1, d_inner)),
            _resident((LANES, d_inner)),
        ],
        out_specs=pl.BlockSpec((None, L, d_inner), tok),
        out_shape=jax.ShapeDtypeStruct((B, S, d_inner), BF16),
        scratch_shapes=[pltpu.VMEM((SSD_D_STATE, d_inner), F32)],
        compiler_params=_params("arbitrary", "arbitrary"),
        name="ssd_core",
    )(xs, bm, cm, dt, z, a_log, d_skip, gnorm, expand)


def kernel(x, mem, positions, ffn1_norm, ffn1_w_gate, ffn1_w_up, ffn1_w_down, mix_norm, da_w_in, da_lambda_q1, da_lambda_k1, da_lambda_q2, da_lambda_k2, da_subln, da_w_out, ssd_w_in, ssd_conv_w, ssd_conv_b, ssd_dt_bias, ssd_A_log, ssd_D, ssd_gnorm, ssd_w_out, xa_norm, xa_mem_norm, xa_w_q, xa_w_kv, xa_w_o, ffn2_norm, ffn2_w_gate, ffn2_w_up, ffn2_w_down, final_norm):
    B, S, D = x.shape
    depth = ffn1_norm.shape[0]
    T = B * S
    row = lambda v: v.reshape(1, -1).astype(F32)
    bf = lambda w: w.astype(BF16)

    inv = 1.0 / (ROPE_THETA ** (jnp.arange(0, DA_HEAD_DIM, 2, dtype=F32) / DA_HEAD_DIM))
    invf = jnp.tile(inv, LANES // inv.shape[0]).reshape(1, LANES)
    lane = jnp.arange(LANES)
    sign = jnp.where((lane % DA_HEAD_DIM) < DA_HEAD_DIM // 2, -1.0, 1.0).astype(F32).reshape(1, LANES)
    pos = positions.astype(F32).reshape(B, 1, S)

    mem_k, mem_v = _mem_kv(mem, xa_mem_norm.reshape(depth, 1, D), bf(xa_w_kv))

    h = x
    for i in range(depth):
        h = _ffn(h.reshape(T, D), row(ffn1_norm[i]), bf(ffn1_w_gate[i]), bf(ffn1_w_up[i]),
                 bf(ffn1_w_down[i]), row(final_norm), final=False).reshape(B, S, D)
        j = i // N_MIXERS
        if i % N_MIXERS == 0:
            q, k, v = _da_qkv(h, pos, row(mix_norm[i]), bf(da_w_in[j]), invf, sign)
            lam_vecs = jnp.stack([da_lambda_q1[j], da_lambda_k1[j], da_lambda_q2[j], da_lambda_k2[j]]).astype(F32)
            mixed = _da_attn(q, k, v, lam_vecs, da_subln[j].astype(F32).reshape(LANES, 1),
                             lambda_init=_lambda_init(i))
            w_mix_out = bf(da_w_out[j])
        else:
            d_inner = ssd_w_out.shape[1]
            n_heads = ssd_A_log.shape[1]
            conv_dim = ssd_conv_w.shape[2]
            w_in = ssd_w_in[j]
            wz = bf(w_in[:, :d_inner])
            wx = bf(w_in[:, d_inner:d_inner + conv_dim])
            wdt = bf(jnp.pad(w_in[:, d_inner + conv_dim:], ((0, 0), (0, LANES - n_heads))))
            dt_bias = jnp.pad(ssd_dt_bias[j].astype(F32), (0, LANES - n_heads)).reshape(1, LANES)
            z, xs, bm, cm, dt = _ssd_in(h, row(mix_norm[i]), wz, wx, wdt, ssd_conv_w[j].astype(F32),
                                        row(ssd_conv_b[j]), dt_bias)
            a_log = jnp.pad(ssd_A_log[j].astype(F32), (0, LANES - n_heads)).reshape(1, LANES)
            d_skip = jnp.repeat(ssd_D[j].astype(F32), SSD_HEAD_DIM).reshape(1, d_inner)
            expand = (jnp.arange(LANES)[:, None] == (jnp.arange(d_inner) // SSD_HEAD_DIM)[None, :]).astype(BF16)
            mixed = _ssd_core(xs, bm, cm, dt, z, a_log, d_skip, row(ssd_gnorm[j]), expand)
            w_mix_out = bf(ssd_w_out[j])
        h = _xattn(h, mixed, w_mix_out, row(xa_norm[i]), bf(xa_w_q[i]), mem_k[i], mem_v[i], bf(xa_w_o[i]))
        h = _ffn(h.reshape(T, D), row(ffn2_norm[i]), bf(ffn2_w_gate[i]), bf(ffn2_w_up[i]),
                 bf(ffn2_w_down[i]), row(final_norm), final=(i == depth - 1)).reshape(B, S, D)
    return h
```

```python
import functools
import math

import jax
import jax.numpy as jnp
from jax import lax
from jax.experimental import pallas as pl
from jax.experimental.pallas import tpu as pltpu

F32 = jnp.float32
BF16 = jnp.bfloat16

EPS = 1e-6
ROPE_THETA = 10000.0
DA_HEAD_DIM = 64
SSD_HEAD_DIM = 64
SSD_N_GROUPS = 8
SSD_D_STATE = 128
SSD_CONV_WIDTH = 4
SSD_CHUNK = 128
XA_N_HEADS = 4
N_MIXERS = 2

LANES = 128
SUBLANES = 8
BF16_ROWS = 16
MXU_DIM = 256
VMEM_LIMIT = 56 * 1024 * 1024


def _lambda_init(layer_idx):
    return 0.8 - 0.6 * math.exp(-0.3 * layer_idx)


def _params(*semantics):
    return pltpu.CompilerParams(dimension_semantics=semantics, vmem_limit_bytes=VMEM_LIMIT)


def _resident(shape):
    zeros = (0,) * len(shape)
    return pl.BlockSpec(shape, lambda *_: zeros, pipeline_mode=pl.Buffered(1))


def _rmsnorm(x, w):
    return x * lax.rsqrt(jnp.mean(x * x, axis=-1, keepdims=True) + EPS) * w


def _dot(a, b):
    return jnp.dot(a, b, preferred_element_type=F32)


def _dot_nt(a, b):
    return lax.dot_general(a, b, (((1,), (1,)), ((), ())), preferred_element_type=F32)


def _split3(x):
    hi = x.astype(BF16)
    r1 = x - hi.astype(F32)
    mid = r1.astype(BF16)
    lo = (r1 - mid.astype(F32)).astype(BF16)
    return hi, mid, lo


def _ffn_kernel(h_ref, nw_ref, wg_ref, wu_ref, wd_ref, fw_ref, o_ref, *, f_chunks, final):
    h = h_ref[...]
    n = _rmsnorm(h, nw_ref[...]).astype(BF16)
    acc = None
    for c0, c1 in f_chunks:
        g = _dot(n, wg_ref[:, c0:c1])
        u = _dot(n, wu_ref[:, c0:c1])
        a = (g * jax.nn.sigmoid(g) * u).astype(BF16)
        d = _dot(a, wd_ref[c0:c1, :])
        acc = d if acc is None else acc + d
    out = h + 0.5 * acc
    if final:
        out = _rmsnorm(out, fw_ref[...])
    o_ref[...] = out


def _ffn(h, norm_w, wg, wu, wd, final_w, *, final, tm=512, fc=1024):
    T, D = h.shape
    F = wg.shape[1]
    tm = min(tm, T)
    f_chunks = tuple((c, min(c + fc, F)) for c in range(0, F, fc))
    return pl.pallas_call(
        functools.partial(_ffn_kernel, f_chunks=f_chunks, final=final),
        grid=(T // tm,),
        in_specs=[
            pl.BlockSpec((tm, D), lambda i: (i, 0)),
            _resident((1, D)),
            _resident((D, F)),
            _resident((D, F)),
            _resident((F, D)),
            _resident((1, D)),
        ],
        out_specs=pl.BlockSpec((tm, D), lambda i: (i, 0)),
        out_shape=jax.ShapeDtypeStruct((T, D), F32),
        compiler_params=_params("arbitrary"),
        name="ffn_final" if final else "ffn",
    )(h, norm_w, wg, wu, wd, final_w)


def _da_qkv_kernel(h_ref, pos_ref, nw_ref, w_ref, invf_ref, sign_ref, q_ref, k_ref, v_ref, *, width):
    tm = h_ref.shape[0]
    n = _rmsnorm(h_ref[...], nw_ref[...]).astype(BF16)
    pos = jnp.broadcast_to(pos_ref[...], (LANES, tm)).T
    ang = pos * invf_ref[...]
    cos = jnp.cos(ang)
    sin = jnp.sin(ang) * sign_ref[...]
    lane = lax.broadcasted_iota(jnp.int32, (tm, LANES), 1)
    low_half = (lane % DA_HEAD_DIM) < (DA_HEAD_DIM // 2)
    half = DA_HEAD_DIM // 2

    def rope(x):
        partner = jnp.where(low_half, pltpu.roll(x, LANES - half, axis=1), pltpu.roll(x, half, axis=1))
        return x * cos + partner * sin

    q_scale = DA_HEAD_DIM ** -0.5 * math.log2(math.e)
    q = _dot(n, w_ref[:, :width])
    k = _dot(n, w_ref[:, width:2 * width])
    for c in range(width // LANES):
        sl = slice(c * LANES, (c + 1) * LANES)
        q_ref[:, sl] = (rope(q[:, sl]) * q_scale).astype(BF16)
        k_ref[:, sl] = rope(k[:, sl]).astype(BF16)
    v_ref[...] = _dot(n, w_ref[:, 2 * width:3 * width]).astype(BF16)


def _da_qkv(h, pos, norm_w, w_in, invf, sign, *, tm=512):
    B, S, D = h.shape
    width = w_in.shape[1] // 3
    tm = min(tm, S)
    tok = lambda b, i: (b, i, 0)
    out = jax.ShapeDtypeStruct((B, S, width), BF16)
    return pl.pallas_call(
        functools.partial(_da_qkv_kernel, width=width),
        grid=(B, S // tm),
        in_specs=[
            pl.BlockSpec((None, tm, D), tok),
            pl.BlockSpec((None, 1, tm), lambda b, i: (b, 0, i)),
            _resident((1, D)),
            _resident((D, 3 * width)),
            _resident((1, LANES)),
            _resident((1, LANES)),
        ],
        out_specs=[pl.BlockSpec((None, tm, width), tok)] * 3,
        out_shape=[out, out, out],
        compiler_params=_params("arbitrary", "arbitrary"),
        name="da_qkv",
    )(h, pos, norm_w, w_in, invf, sign)


def _transpose_bf16(x):
    return x.astype(F32).T.astype(BF16)


def _da_attn_kernel(q_ref, k_ref, v_ref, lam_ref, subln_ref, o_ref, *scratch, tq, heads, lambda_init):
    qi = pl.program_id(2)
    S = k_ref.shape[0]
    head_lanes = [slice(hh * LANES, (hh + 1) * LANES) for hh in range(heads)]
    vt_refs, s_refs, p_refs, acc_refs, st_refs = (scratch[i * heads:(i + 1) * heads] for i in range(5))

    @pl.when(qi == 0)
    def _():
        for hh, hl in enumerate(head_lanes):
            vt_refs[hh][LANES:, :] = jnp.ones((BF16_ROWS, S), BF16)
            for c in range(0, S, tq):
                vt_refs[hh][:LANES, c:c + tq] = _transpose_bf16(v_ref[c:c + tq, hl])

    sub = lax.broadcasted_iota(jnp.int32, (LANES, tq), 0)
    qst = []
    for hh, hl in enumerate(head_lanes):
        qt = _transpose_bf16(q_ref[:, hl])
        zero = jnp.zeros_like(qt)
        qst.append(jnp.concatenate([jnp.where(sub < DA_HEAD_DIM, qt, zero),
                                    jnp.where(sub >= DA_HEAD_DIM, qt, zero)], axis=1))
        acc_refs[hh][...] = jnp.zeros_like(acc_refs[hh])
        st_refs[hh][0:1, :] = jnp.full((1, 2 * tq), -jnp.inf, F32)

    key = lax.broadcasted_iota(jnp.int32, (tq, LANES), 0)
    lane = lax.broadcasted_iota(jnp.int32, (tq, LANES), 1)

    def scores(hh, j):
        start = pl.multiple_of(j * tq, tq)
        s = _dot(k_ref[pl.ds(start, tq), head_lanes[hh]], qst[hh])
        s_refs[hh][...] = s
        st_refs[hh][2:3, :] = jnp.max(s, axis=0, keepdims=True)

    def softmax(hh, diagonal):
        s_ref, p_ref, st_ref = s_refs[hh], p_refs[hh], st_refs[hh]
        for c in range(0, 2 * tq, LANES):
            cs = slice(c, c + LANES)

            def strip():
                s = s_ref[:, cs]
                return jnp.where(key <= lane + (c % tq), s, -jnp.inf) if diagonal else s

            m_old = st_ref[0:1, cs]
            m_blk = jnp.max(strip(), axis=0, keepdims=True) if diagonal else st_ref[2:3, cs]
            m_new = jnp.maximum(m_old, m_blk)
            st_ref[0:1, cs] = m_new
            st_ref[1:2, cs] = jnp.exp2(m_old - m_new)
            p_ref[:, cs] = jnp.exp2((strip() - m_new).astype(BF16))

    def weighted_values(hh, j):
        start = pl.multiple_of(j * tq, tq)
        pv = _dot(vt_refs[hh][:, pl.ds(start, tq)], p_refs[hh][...])
        acc_refs[hh][...] = st_refs[hh][1:2, :] * acc_refs[hh][...] + pv

    def step(j, diagonal):
        for hh in range(heads):
            if hh + 1 < heads:
                scores(hh + 1, j)
            elif not diagonal:
                scores(0, j + 1)
            softmax(hh, diagonal)
            weighted_values(hh, j)

    def body(j, carry):
        step(j, False)
        return carry

    scores(0, 0)
    lax.fori_loop(0, qi, body, 0)
    step(qi, True)

    lam_v = lam_ref[...]
    lam = (jnp.exp(jnp.sum(lam_v[0:1] * lam_v[1:2], axis=-1, keepdims=True))
           - jnp.exp(jnp.sum(lam_v[2:3] * lam_v[3:4], axis=-1, keepdims=True))
           + lambda_init)
    for hh, hl in enumerate(head_lanes):
        o = acc_refs[hh][:LANES, :] / acc_refs[hh][LANES:LANES + 1, :]
        d = o[:, :tq] - lam * o[:, tq:]
        y = d * lax.rsqrt(jnp.mean(d * d, axis=0, keepdims=True) + EPS) * subln_ref[...]
        o_ref[:, hl] = (y * (1.0 - lambda_init)).T.astype(BF16)


def _da_attn(q, k, v, lam_vecs, subln_col, *, lambda_init, tq=512, heads=2):
    B, S, W = q.shape
    H = W // LANES
    tq = min(tq, S)
    hw = heads * LANES
    per_head = lambda shape, dtype: [pltpu.VMEM(shape, dtype) for _ in range(heads)]
    return pl.pallas_call(
        functools.partial(_da_attn_kernel, tq=tq, heads=heads, lambda_init=lambda_init),
        grid=(B, H // heads, S // tq),
        in_specs=[
            pl.BlockSpec((None, tq, hw), lambda b, h, i: (b, i, h)),
            pl.BlockSpec((None, S, hw), lambda b, h, i: (b, 0, h), pipeline_mode=pl.Buffered(1)),
            pl.BlockSpec((None, S, hw), lambda b, h, i: (b, 0, h), pipeline_mode=pl.Buffered(1)),
            _resident(lam_vecs.shape),
            _resident((LANES, 1)),
        ],
        out_specs=pl.BlockSpec((None, tq, hw), lambda b, h, i: (b, i, h)),
        out_shape=jax.ShapeDtypeStruct((B, S, W), BF16),
        scratch_shapes=(per_head((LANES + BF16_ROWS, S), BF16) + per_head((tq, 2 * tq), F32)
                        + per_head((tq, 2 * tq), BF16) + per_head((LANES + BF16_ROWS, 2 * tq), F32)
                        + per_head((SUBLANES, 2 * tq), F32)),
        compiler_params=_params("arbitrary", "arbitrary", "arbitrary"),
        name="da_attn",
    )(q, k, v, lam_vecs, subln_col)


def _mem_kv_kernel(mem_ref, nw_ref, w_ref, k_ref, v_ref):
    D = mem_ref.shape[-1]
    m = _rmsnorm(mem_ref[...], nw_ref[...]).astype(BF16)
    kv = _dot(m, w_ref[...])
    k_ref[...] = kv[:, :D].astype(BF16)
    v_ref[...] = kv[:, D:].astype(BF16)


def _mem_kv(mem, norm_w, w_kv):
    B, M, D = mem.shape
    L = w_kv.shape[0]
    out = jax.ShapeDtypeStruct((L, B, M, D), BF16)
    return pl.pallas_call(
        _mem_kv_kernel,
        grid=(L, B),
        in_specs=[
            pl.BlockSpec((None, M, D), lambda l, b: (b, 0, 0)),
            pl.BlockSpec((None, 1, D), lambda l, b: (l, 0, 0)),
            pl.BlockSpec((None, D, 2 * D), lambda l, b: (l, 0, 0)),
        ],
        out_specs=[pl.BlockSpec((None, None, M, D), lambda l, b: (l, b, 0, 0))] * 2,
        out_shape=[out, out],
        compiler_params=_params("arbitrary", "arbitrary"),
        name="mem_kv",
    )(mem, norm_w, w_kv)


def _xattn_kernel(h_ref, a_ref, wpre_ref, nw_ref, wq_ref, k_ref, v_ref, wo_ref, o_ref, *, n_heads):
    h = h_ref[...] + _dot(a_ref[...], wpre_ref[...])
    D = h.shape[-1]
    hd = D // n_heads
    n = _rmsnorm(h, nw_ref[...]).astype(BF16)
    q = (_dot(n, wq_ref[...]) * (hd ** -0.5)).astype(BF16)
    heads = []
    for i in range(n_heads):
        sl = slice(i * hd, (i + 1) * hd)
        s = _dot_nt(q[:, sl], k_ref[:, sl])
        p = jnp.exp(s - jnp.max(s, axis=-1, keepdims=True))
        l = jnp.sum(p, axis=-1, keepdims=True)
        heads.append((_dot(p.astype(BF16), v_ref[:, sl]) / l).astype(BF16))
    o = jnp.concatenate(heads, axis=-1)
    o_ref[...] = h + _dot(o, wo_ref[...])


def _xattn(h, a, w_pre, norm_w, wq, k, v, wo, *, tm=512):
    B, S, D = h.shape
    Ka = a.shape[-1]
    M = k.shape[1]
    tm = min(tm, S)
    tok = lambda b, i: (b, i, 0)
    return pl.pallas_call(
        functools.partial(_xattn_kernel, n_heads=XA_N_HEADS),
        grid=(B, S // tm),
        in_specs=[
            pl.BlockSpec((None, tm, D), tok),
            pl.BlockSpec((None, tm, Ka), tok),
            _resident((Ka, D)),
            _resident((1, D)),
            _resident((D, D)),
            pl.BlockSpec((None, M, D), lambda b, i: (b, 0, 0)),
            pl.BlockSpec((None, M, D), lambda b, i: (b, 0, 0)),
            _resident((D, D)),
        ],
        out_specs=pl.BlockSpec((None, tm, D), tok),
        out_shape=jax.ShapeDtypeStruct((B, S, D), F32),
        compiler_params=_params("arbitrary", "arbitrary"),
        name="xattn",
    )(h, a, w_pre, norm_w, wq, k, v, wo)


def _softplus(x):
    return jnp.maximum(x, 0.0) + jnp.log(1.0 + jnp.exp(-jnp.abs(x)))


def _ssd_in_kernel(h_ref, nw_ref, wz_ref, wx_ref, wdt_ref, cw_ref, cb_ref, dtb_ref,
                   z_ref, xs_ref, b_ref, c_ref, dt_ref, u_ref, n_ref, *, d_inner, bc_width, col_chunk):
    tm = h_ref.shape[0]

    @pl.when(pl.program_id(1) == 0)
    def _():
        u_ref[:SUBLANES, :] = jnp.zeros((SUBLANES, u_ref.shape[1]), F32)

    n_ref[...] = _rmsnorm(h_ref[...], nw_ref[...]).astype(BF16)
    dt_ref[...] = _softplus(_dot(n_ref[...], wdt_ref[...]) + dtb_ref[...])

    conv_dim = wx_ref.shape[1]
    n_chunks = conv_dim // col_chunk
    z_chunk = d_inner // n_chunks

    def project(c):
        u_ref[SUBLANES:, c * col_chunk:(c + 1) * col_chunk] = _dot(
            n_ref[...], wx_ref[:, c * col_chunk:(c + 1) * col_chunk])
        z_ref[:, c * z_chunk:(c + 1) * z_chunk] = _dot(n_ref[...], wz_ref[:, c * z_chunk:(c + 1) * z_chunk])

    project(0)
    for c in range(n_chunks):
        if c + 1 < n_chunks:
            project(c + 1)
        for c0 in range(c * col_chunk, (c + 1) * col_chunk, LANES):
            cs = slice(c0, c0 + LANES)
            w = cw_ref[:, cs]
            acc = cb_ref[:, cs]
            for k in range(SSD_CONV_WIDTH):
                shifted = u_ref[SUBLANES - k:SUBLANES - k + tm, cs]
                acc = acc + shifted * w[SSD_CONV_WIDTH - 1 - k:SSD_CONV_WIDTH - k]
            u_ref[:SUBLANES, cs] = u_ref[tm:tm + SUBLANES, cs]
            y = acc * jax.nn.sigmoid(acc)
            if c0 < d_inner:
                xs_ref[:, cs] = y
            elif c0 < d_inner + bc_width:
                b_ref[:, c0 - d_inner:c0 - d_inner + LANES] = y.astype(BF16)
            else:
                c_ref[:, c0 - d_inner - bc_width:c0 - d_inner - bc_width + LANES] = y.astype(BF16)


def _ssd_in(h, norm_w, wz, wx, wdt, conv_w, conv_b, dt_bias, *, tm=256, col_chunk=512):
    B, S, D = h.shape
    d_inner = wz.shape[1]
    conv_dim = wx.shape[1]
    bc_width = (conv_dim - d_inner) // 2
    tm = min(tm, S)
    tok = lambda b, i: (b, i, 0)
    return pl.pallas_call(
        functools.partial(_ssd_in_kernel, d_inner=d_inner, bc_width=bc_width, col_chunk=col_chunk),
        grid=(B, S // tm),
        in_specs=[
            pl.BlockSpec((None, tm, D), tok),
            _resident((1, D)),
            _resident((D, d_inner)),
            _resident((D, conv_dim)),
            _resident((D, LANES)),
            _resident((SSD_CONV_WIDTH, conv_dim)),
            _resident((1, conv_dim)),
            _resident((1, LANES)),
        ],
        out_specs=[
            pl.BlockSpec((None, tm, d_inner), tok),
            pl.BlockSpec((None, tm, d_inner), tok),
            pl.BlockSpec((None, tm, bc_width), tok),
            pl.BlockSpec((None, tm, bc_width), tok),
            pl.BlockSpec((None, tm, LANES), tok),
        ],
        out_shape=[
            jax.ShapeDtypeStruct((B, S, d_inner), F32),
            jax.ShapeDtypeStruct((B, S, d_inner), F32),
            jax.ShapeDtypeStruct((B, S, bc_width), BF16),
            jax.ShapeDtypeStruct((B, S, bc_width), BF16),
            jax.ShapeDtypeStruct((B, S, LANES), F32),
        ],
        scratch_shapes=[pltpu.VMEM((SUBLANES + tm, conv_dim), F32), pltpu.VMEM((tm, D), BF16)],
        compiler_params=_params("arbitrary", "arbitrary"),
        name="ssd_in",
    )(h, norm_w, wz, wx, wdt, conv_w, conv_b, dt_bias)


def _ssd_core_kernel(xs_ref, b_ref, c_ref, dt_ref, z_ref, alog_ref, dskip_ref, gn_ref, expand_ref,
                     y_ref, state_ref, *, n_groups, heads_per_group):
    L = xs_ref.shape[0]
    gw = heads_per_group * SSD_HEAD_DIM

    @pl.when(pl.program_id(1) == 0)
    def _():
        state_ref[...] = jnp.zeros_like(state_ref)

    row = lax.broadcasted_iota(jnp.int32, (L, L), 0)
    col = lax.broadcasted_iota(jnp.int32, (L, L), 1)
    causal = row >= col
    tril = causal.astype(BF16)
    expand = expand_ref[...]

    dt = dt_ref[...]
    a = dt * -jnp.exp(alog_ref[...])
    a_cs = sum(_dot(tril, part) for part in _split3(a))
    a_cs_t = a_cs.T
    dt_full = sum(_dot(part, expand) for part in _split3(dt)[:2])
    acs_full = sum(_dot(part, expand) for part in _split3(a_cs)[:2])
    a_end = acs_full[L - 1:L, :]

    x = xs_ref[...]
    x_dt = x * dt_full
    x_end = (x_dt * jnp.exp(a_end - acs_full)).astype(BF16)
    carry_in = jnp.exp(acs_full)
    chunk_decay = jnp.exp(a_end)
    head_of_lane = lax.broadcasted_iota(jnp.int32, (L, gw), 1) // SSD_HEAD_DIM

    for g in range(n_groups):
        gs = slice(g * gw, (g + 1) * gw)
        ns = slice(g * SSD_D_STATE, (g + 1) * SSD_D_STATE)
        b_g = b_ref[:, ns]
        c_g = c_ref[:, ns]
        cb = _dot_nt(c_g, b_g)
        state = state_ref[:, gs]
        y = _dot(c_g, state.astype(BF16)) * carry_in[:, gs]
        x_dt_g = x_dt[:, gs]
        mixes, inputs = [], []
        for r in range(heads_per_group):
            hh = g * heads_per_group + r
            seg = a_cs[:, hh:hh + 1] - a_cs_t[hh:hh + 1, :]
            decay = jnp.exp(jnp.where(causal, seg, -jnp.inf))
            mixes.append((cb * decay).astype(BF16))
            inputs.append(jnp.where(head_of_lane == r, x_dt_g, 0.0).astype(BF16))
        y = y + _dot(jnp.concatenate(mixes, axis=1), jnp.concatenate(inputs, axis=0))
        b_t = b_g.astype(F32).T.astype(BF16)
        state_ref[:, gs] = state * chunk_decay[:, gs] + _dot(b_t, x_end[:, gs])

        y = y + dskip_ref[:, gs] * x[:, gs]
        zg = z_ref[:, gs]
        y = y * (zg * jax.nn.sigmoid(zg))
        y = y * lax.rsqrt(jnp.mean(y * y, axis=-1, keepdims=True) + EPS) * gn_ref[:, gs]
        y_ref[:, gs] = y.astype(BF16)


def _ssd_core(xs, bm, cm, dt, z, a_log, d_skip, gnorm, expand):
    B, S, d_inner = xs.shape
    n_groups = SSD_N_GROUPS
    heads_per_group = d_inner // SSD_HEAD_DIM // n_groups
    bc_width = bm.shape[-1]
    L = SSD_CHUNK
    tok = lambda b, i: (b, i, 0)
    return pl.pallas_call(
        functools.partial(_ssd_core_kernel, n_groups=n_groups, heads_per_group=heads_per_group),
        grid=(B, S // L),
        in_specs=[
            pl.BlockSpec((None, L, d_inner), tok),
            pl.BlockSpec((None, L, bc_width), tok),
            pl.BlockSpec((None, L, bc_width), tok),
            pl.BlockSpec((None, L, LANES), tok),
            pl.BlockSpec((None, L, d_inner), tok),
            _resident((1, LANES)),
            _resident((1, d_inner)),
            _resident((1, d_inner)),
            _resident((LANES, d_inner)),
        ],
        out_specs=pl.BlockSpec((None, L, d_inner), tok),
        out_shape=jax.ShapeDtypeStruct((B, S, d_inner), BF16),
        scratch_shapes=[pltpu.VMEM((SSD_D_STATE, d_inner), F32)],
        compiler_params=_params("arbitrary", "arbitrary"),
        name="ssd_core",
    )(xs, bm, cm, dt, z, a_log, d_skip, gnorm, expand)


def kernel(x, mem, positions, ffn1_norm, ffn1_w_gate, ffn1_w_up, ffn1_w_down, mix_norm, da_w_in, da_lambda_q1, da_lambda_k1, da_lambda_q2, da_lambda_k2, da_subln, da_w_out, ssd_w_in, ssd_conv_w, ssd_conv_b, ssd_dt_bias, ssd_A_log, ssd_D, ssd_gnorm, ssd_w_out, xa_norm, xa_mem_norm, xa_w_q, xa_w_kv, xa_w_o, ffn2_norm, ffn2_w_gate, ffn2_w_up, ffn2_w_down, final_norm):
    B, S, D = x.shape
    depth = ffn1_norm.shape[0]
    T = B * S
    row = lambda v: v.reshape(1, -1).astype(F32)
    bf = lambda w: w.astype(BF16)

    inv = 1.0 / (ROPE_THETA ** (jnp.arange(0, DA_HEAD_DIM, 2, dtype=F32) / DA_HEAD_DIM))
    invf = jnp.tile(inv, LANES // inv.shape[0]).reshape(1, LANES)
    lane = jnp.arange(LANES)
    sign = jnp.where((lane % DA_HEAD_DIM) < DA_HEAD_DIM // 2, -1.0, 1.0).astype(F32).reshape(1, LANES)
    pos = positions.astype(F32).reshape(B, 1, S)

    mem_k, mem_v = _mem_kv(mem, xa_mem_norm.reshape(depth, 1, D), bf(xa_w_kv))

    h = x
    for i in range(depth):
        h = _ffn(h.reshape(T, D), row(ffn1_norm[i]), bf(ffn1_w_gate[i]), bf(ffn1_w_up[i]),
                 bf(ffn1_w_down[i]), row(final_norm), final=False).reshape(B, S, D)
        j = i // N_MIXERS
        if i % N_MIXERS == 0:
            q, k, v = _da_qkv(h, pos, row(mix_norm[i]), bf(da_w_in[j]), invf, sign)
            lam_vecs = jnp.stack([da_lambda_q1[j], da_lambda_k1[j], da_lambda_q2[j], da_lambda_k2[j]]).astype(F32)
            mixed = _da_attn(q, k, v, lam_vecs, da_subln[j].astype(F32).reshape(LANES, 1),
                             lambda_init=_lambda_init(i))
            w_mix_out = bf(da_w_out[j])
        else:
            d_inner = ssd_w_out.shape[1]
            n_heads = ssd_A_log.shape[1]
            conv_dim = ssd_conv_w.shape[2]
            w_in = ssd_w_in[j]
            wz = bf(w_in[:, :d_inner])
            wx = bf(w_in[:, d_inner:d_inner + conv_dim])
            wdt = bf(jnp.pad(w_in[:, d_inner + conv_dim:], ((0, 0), (0, LANES - n_heads))))
            dt_bias = jnp.pad(ssd_dt_bias[j].astype(F32), (0, LANES - n_heads)).reshape(1, LANES)
            z, xs, bm, cm, dt = _ssd_in(h, row(mix_norm[i]), wz, wx, wdt, ssd_conv_w[j].astype(F32),
                                        row(ssd_conv_b[j]), dt_bias)
            a_log = jnp.pad(ssd_A_log[j].astype(F32), (0, LANES - n_heads)).reshape(1, LANES)
            d_skip = jnp.repeat(ssd_D[j].astype(F32), SSD_HEAD_DIM).reshape(1, d_inner)
            expand = (jnp.arange(LANES)[:, None] == (jnp.arange(d_inner) // SSD_HEAD_DIM)[None, :]).astype(BF16)
            mixed = _ssd_core(xs, bm, cm, dt, z, a_log, d_skip, row(ssd_gnorm[j]), expand)
            w_mix_out = bf(ssd_w_out[j])
        h = _xattn(h, mixed, w_mix_out, row(xa_norm[i]), bf(xa_w_q[i]), mem_k[i], mem_v[i], bf(xa_w_o[i]))
        h = _ffn(h.reshape(T, D), row(ffn2_norm[i]), bf(ffn2_w_gate[i]), bf(ffn2_w_up[i]),
                 bf(ffn2_w_down[i]), row(final_norm), final=(i == depth - 1)).reshape(B, S, D)
    return h
```

```python
import functools
import math

import jax
import jax.numpy as jnp
from jax import lax
from jax.experimental import pallas as pl
from jax.experimental.pallas import tpu as pltpu

F32 = jnp.float32
BF16 = jnp.bfloat16

EPS = 1e-6
ROPE_THETA = 10000.0
DA_HEAD_DIM = 64
SSD_HEAD_DIM = 64
SSD_N_GROUPS = 8
SSD_D_STATE = 128
SSD_CONV_WIDTH = 4
SSD_CHUNK = 128
XA_N_HEADS = 4
N_MIXERS = 2

LANES = 128
SUBLANES = 8
BF16_ROWS = 16
MXU_DIM = 256
VMEM_LIMIT = 56 * 1024 * 1024


def _lambda_init(layer_idx):
    return 0.8 - 0.6 * math.exp(-0.3 * layer_idx)


def _params(*semantics):
    return pltpu.CompilerParams(dimension_semantics=semantics, vmem_limit_bytes=VMEM_LIMIT)


def _resident(shape):
    zeros = (0,) * len(shape)
    return pl.BlockSpec(shape, lambda *_: zeros, pipeline_mode=pl.Buffered(1))


def _rmsnorm(x, w):
    return x * lax.rsqrt(jnp.mean(x * x, axis=-1, keepdims=True) + EPS) * w


def _dot(a, b):
    return jnp.dot(a, b, preferred_element_type=F32)


def _dot_nt(a, b):
    return lax.dot_general(a, b, (((1,), (1,)), ((), ())), preferred_element_type=F32)


def _split3(x):
    hi = x.astype(BF16)
    r1 = x - hi.astype(F32)
    mid = r1.astype(BF16)
    lo = (r1 - mid.astype(F32)).astype(BF16)
    return hi, mid, lo


def _ffn_kernel(h_ref, nw_ref, wg_ref, wu_ref, wd_ref, fw_ref, o_ref, *, f_chunks, final):
    h = h_ref[...]
    n = _rmsnorm(h, nw_ref[...]).astype(BF16)
    acc = None
    for c0, c1 in f_chunks:
        g = _dot(n, wg_ref[:, c0:c1])
        u = _dot(n, wu_ref[:, c0:c1])
        a = (g * jax.nn.sigmoid(g) * u).astype(BF16)
        d = _dot(a, wd_ref[c0:c1, :])
        acc = d if acc is None else acc + d
    out = h + 0.5 * acc
    if final:
        out = _rmsnorm(out, fw_ref[...])
    o_ref[...] = out


def _ffn(h, norm_w, wg, wu, wd, final_w, *, final, tm=512, fc=1024):
    T, D = h.shape
    F = wg.shape[1]
    tm = min(tm, T)
    f_chunks = tuple((c, min(c + fc, F)) for c in range(0, F, fc))
    return pl.pallas_call(
        functools.partial(_ffn_kernel, f_chunks=f_chunks, final=final),
        grid=(T // tm,),
        in_specs=[
            pl.BlockSpec((tm, D), lambda i: (i, 0)),
            _resident((1, D)),
            _resident((D, F)),
            _resident((D, F)),
            _resident((F, D)),
            _resident((1, D)),
        ],
        out_specs=pl.BlockSpec((tm, D), lambda i: (i, 0)),
        out_shape=jax.ShapeDtypeStruct((T, D), F32),
        compiler_params=_params("arbitrary"),
        name="ffn_final" if final else "ffn",
    )(h, norm_w, wg, wu, wd, final_w)


def _da_qkv_kernel(h_ref, pos_ref, nw_ref, w_ref, invf_ref, sign_ref, q_ref, k_ref, v_ref, *, width):
    tm = h_ref.shape[0]
    n = _rmsnorm(h_ref[...], nw_ref[...]).astype(BF16)
    pos = jnp.broadcast_to(pos_ref[...], (LANES, tm)).T
    ang = pos * invf_ref[...]
    cos = jnp.cos(ang)
    sin = jnp.sin(ang) * sign_ref[...]
    lane = lax.broadcasted_iota(jnp.int32, (tm, LANES), 1)
    low_half = (lane % DA_HEAD_DIM) < (DA_HEAD_DIM // 2)
    half = DA_HEAD_DIM // 2

    def rope(x):
        partner = jnp.where(low_half, pltpu.roll(x, LANES - half, axis=1), pltpu.roll(x, half, axis=1))
        return x * cos + partner * sin

    q_scale = DA_HEAD_DIM ** -0.5 * math.log2(math.e)
    q = _dot(n, w_ref[:, :width])
    k = _dot(n, w_ref[:, width:2 * width])
    for c in range(width // LANES):
        sl = slice(c * LANES, (c + 1) * LANES)
        q_ref[:, sl] = (rope(q[:, sl]) * q_scale).astype(BF16)
        k_ref[:, sl] = rope(k[:, sl]).astype(BF16)
    v_ref[...] = _dot(n, w_ref[:, 2 * width:3 * width]).astype(BF16)


def _da_qkv(h, pos, norm_w, w_in, invf, sign, *, tm=512):
    B, S, D = h.shape
    width = w_in.shape[1] // 3
    tm = min(tm, S)
    tok = lambda b, i: (b, i, 0)
    out = jax.ShapeDtypeStruct((B, S, width), BF16)
    return pl.pallas_call(
        functools.partial(_da_qkv_kernel, width=width),
        grid=(B, S // tm),
        in_specs=[
            pl.BlockSpec((None, tm, D), tok),
            pl.BlockSpec((None, 1, tm), lambda b, i: (b, 0, i)),
            _resident((1, D)),
            _resident((D, 3 * width)),
            _resident((1, LANES)),
            _resident((1, LANES)),
        ],
        out_specs=[pl.BlockSpec((None, tm, width), tok)] * 3,
        out_shape=[out, out, out],
        compiler_params=_params("arbitrary", "arbitrary"),
        name="da_qkv",
    )(h, pos, norm_w, w_in, invf, sign)


def _transpose_bf16(x):
    return x.astype(F32).T.astype(BF16)


def _da_attn_kernel(q_ref, k_ref, v_ref, lam_ref, subln_ref, o_ref, *scratch, tq, heads, lambda_init):
    qi = pl.program_id(2)
    S = k_ref.shape[0]
    head_lanes = [slice(hh * LANES, (hh + 1) * LANES) for hh in range(heads)]
    vt_refs, s_refs, p_refs, acc_refs, st_refs = (scratch[i * heads:(i + 1) * heads] for i in range(5))

    @pl.when(qi == 0)
    def _():
        for hh, hl in enumerate(head_lanes):
            vt_refs[hh][LANES:, :] = jnp.ones((BF16_ROWS, S), BF16)
            for c in range(0, S, tq):
                vt_refs[hh][:LANES, c:c + tq] = _transpose_bf16(v_ref[c:c + tq, hl])

    sub = lax.broadcasted_iota(jnp.int32, (LANES, tq), 0)
    qst = []
    for hh, hl in enumerate(head_lanes):
        qt = _transpose_bf16(q_ref[:, hl])
        zero = jnp.zeros_like(qt)
        qst.append(jnp.concatenate([jnp.where(sub < DA_HEAD_DIM, qt, zero),
                                    jnp.where(sub >= DA_HEAD_DIM, qt, zero)], axis=1))
        acc_refs[hh][...] = jnp.zeros_like(acc_refs[hh])
        st_refs[hh][0:1, :] = jnp.full((1, 2 * tq), -jnp.inf, F32)

    key = lax.broadcasted_iota(jnp.int32, (tq, LANES), 0)
    lane = lax.broadcasted_iota(jnp.int32, (tq, LANES), 1)

    def scores(hh, j):
        start = pl.multiple_of(j * tq, tq)
        s = _dot(k_ref[pl.ds(start, tq), head_lanes[hh]], qst[hh])
        s_refs[hh][...] = s
        st_refs[hh][2:3, :] = jnp.max(s, axis=0, keepdims=True)

    def softmax(hh, diagonal):
        s_ref, p_ref, st_ref = s_refs[hh], p_refs[hh], st_refs[hh]
        for c in range(0, 2 * tq, LANES):
            cs = slice(c, c + LANES)

            def strip():
                s = s_ref[:, cs]
                return jnp.where(key <= lane + (c % tq), s, -jnp.inf) if diagonal else s

            m_old = st_ref[0:1, cs]
            m_blk = jnp.max(strip(), axis=0, keepdims=True) if diagonal else st_ref[2:3, cs]
            m_new = jnp.maximum(m_old, m_blk)
            st_ref[0:1, cs] = m_new
            st_ref[1:2, cs] = jnp.exp2(m_old - m_new)
            p_ref[:, cs] = jnp.exp2(strip() - m_new).astype(BF16)

    def weighted_values(hh, j):
        start = pl.multiple_of(j * tq, tq)
        pv = _dot(vt_refs[hh][:, pl.ds(start, tq)], p_refs[hh][...])
        acc_refs[hh][...] = st_refs[hh][1:2, :] * acc_refs[hh][...] + pv

    def step(j, diagonal):
        for hh in range(heads):
            if hh + 1 < heads:
                scores(hh + 1, j)
            elif not diagonal:
                scores(0, j + 1)
            softmax(hh, diagonal)
            weighted_values(hh, j)

    def body(j, carry):
        step(j, False)
        return carry

    scores(0, 0)
    lax.fori_loop(0, qi, body, 0)
    step(qi, True)

    lam_v = lam_ref[...]
    lam = (jnp.exp(jnp.sum(lam_v[0:1] * lam_v[1:2], axis=-1, keepdims=True))
           - jnp.exp(jnp.sum(lam_v[2:3] * lam_v[3:4], axis=-1, keepdims=True))
           + lambda_init)
    for hh, hl in enumerate(head_lanes):
        o = acc_refs[hh][:LANES, :] / acc_refs[hh][LANES:LANES + 1, :]
        d = o[:, :tq] - lam * o[:, tq:]
        y = d * lax.rsqrt(jnp.mean(d * d, axis=0, keepdims=True) + EPS) * subln_ref[...]
        o_ref[:, hl] = (y * (1.0 - lambda_init)).T.astype(BF16)


def _da_attn(q, k, v, lam_vecs, subln_col, *, lambda_init, tq=512, heads=2):
    B, S, W = q.shape
    H = W // LANES
    tq = min(tq, S)
    hw = heads * LANES
    per_head = lambda shape, dtype: [pltpu.VMEM(shape, dtype) for _ in range(heads)]
    return pl.pallas_call(
        functools.partial(_da_attn_kernel, tq=tq, heads=heads, lambda_init=lambda_init),
        grid=(B, H // heads, S // tq),
        in_specs=[
            pl.BlockSpec((None, tq, hw), lambda b, h, i: (b, i, h)),
            pl.BlockSpec((None, S, hw), lambda b, h, i: (b, 0, h)),
            pl.BlockSpec((None, S, hw), lambda b, h, i: (b, 0, h)),
            _resident(lam_vecs.shape),
            _resident((LANES, 1)),
        ],
        out_specs=pl.BlockSpec((None, tq, hw), lambda b, h, i: (b, i, h)),
        out_shape=jax.ShapeDtypeStruct((B, S, W), BF16),
        scratch_shapes=(per_head((LANES + BF16_ROWS, S), BF16) + per_head((tq, 2 * tq), F32)
                        + per_head((tq, 2 * tq), BF16) + per_head((LANES + BF16_ROWS, 2 * tq), F32)
                        + per_head((SUBLANES, 2 * tq), F32)),
        compiler_params=_params("arbitrary", "arbitrary", "arbitrary"),
        name="da_attn",
    )(q, k, v, lam_vecs, subln_col)


def _mem_kv_kernel(mem_ref, nw_ref, w_ref, k_ref, v_ref):
    D = mem_ref.shape[-1]
    m = _rmsnorm(mem_ref[...], nw_ref[...]).astype(BF16)
    kv = _dot(m, w_ref[...])
    k_ref[...] = kv[:, :D].astype(BF16)
    v_ref[...] = kv[:, D:].astype(BF16)


def _mem_kv(mem, norm_w, w_kv):
    B, M, D = mem.shape
    L = w_kv.shape[0]
    out = jax.ShapeDtypeStruct((L, B, M, D), BF16)
    return pl.pallas_call(
        _mem_kv_kernel,
        grid=(L, B),
        in_specs=[
            pl.BlockSpec((None, M, D), lambda l, b: (b, 0, 0)),
            pl.BlockSpec((None, 1, D), lambda l, b: (l, 0, 0)),
            pl.BlockSpec((None, D, 2 * D), lambda l, b: (l, 0, 0)),
        ],
        out_specs=[pl.BlockSpec((None, None, M, D), lambda l, b: (l, b, 0, 0))] * 2,
        out_shape=[out, out],
        compiler_params=_params("arbitrary", "arbitrary"),
        name="mem_kv",
    )(mem, norm_w, w_kv)


def _xattn_kernel(h_ref, a_ref, wpre_ref, nw_ref, wq_ref, k_ref, v_ref, wo_ref, o_ref, *, n_heads):
    h = h_ref[...] + _dot(a_ref[...], wpre_ref[...])
    D = h.shape[-1]
    hd = D // n_heads
    n = _rmsnorm(h, nw_ref[...]).astype(BF16)
    q = (_dot(n, wq_ref[...]) * (hd ** -0.5)).astype(BF16)
    heads = []
    for i in range(n_heads):
        sl = slice(i * hd, (i + 1) * hd)
        s = _dot_nt(q[:, sl], k_ref[:, sl])
        p = jnp.exp(s - jnp.max(s, axis=-1, keepdims=True))
        l = jnp.sum(p, axis=-1, keepdims=True)
        heads.append((_dot(p.astype(BF16), v_ref[:, sl]) / l).astype(BF16))
    o = jnp.concatenate(heads, axis=-1)
    o_ref[...] = h + _dot(o, wo_ref[...])


def _xattn(h, a, w_pre, norm_w, wq, k, v, wo, *, tm=512):
    B, S, D = h.shape
    Ka = a.shape[-1]
    M = k.shape[1]
    tm = min(tm, S)
    tok = lambda b, i: (b, i, 0)
    return pl.pallas_call(
        functools.partial(_xattn_kernel, n_heads=XA_N_HEADS),
        grid=(B, S // tm),
        in_specs=[
            pl.BlockSpec((None, tm, D), tok),
            pl.BlockSpec((None, tm, Ka), tok),
            _resident((Ka, D)),
            _resident((1, D)),
            _resident((D, D)),
            pl.BlockSpec((None, M, D), lambda b, i: (b, 0, 0)),
            pl.BlockSpec((None, M, D), lambda b, i: (b, 0, 0)),
            _resident((D, D)),
        ],
        out_specs=pl.BlockSpec((None, tm, D), tok),
        out_shape=jax.ShapeDtypeStruct((B, S, D), F32),
        compiler_params=_params("arbitrary", "arbitrary"),
        name="xattn",
    )(h, a, w_pre, norm_w, wq, k, v, wo)


def _softplus(x):
    return jnp.maximum(x, 0.0) + jnp.log(1.0 + jnp.exp(-jnp.abs(x)))


def _ssd_in_kernel(h_ref, nw_ref, wz_ref, wx_ref, wdt_ref, cw_ref, cb_ref, dtb_ref,
                   z_ref, xs_ref, b_ref, c_ref, dt_ref, u_ref, n_ref, *, d_inner, bc_width, col_chunk):
    tm = h_ref.shape[0]

    @pl.when(pl.program_id(1) == 0)
    def _():
        u_ref[:SUBLANES, :] = jnp.zeros((SUBLANES, u_ref.shape[1]), F32)

    n_ref[...] = _rmsnorm(h_ref[...], nw_ref[...]).astype(BF16)
    dt_ref[...] = _softplus(_dot(n_ref[...], wdt_ref[...]) + dtb_ref[...])

    conv_dim = wx_ref.shape[1]
    n_chunks = conv_dim // col_chunk
    z_chunk = d_inner // n_chunks

    def project(c):
        u_ref[SUBLANES:, c * col_chunk:(c + 1) * col_chunk] = _dot(
            n_ref[...], wx_ref[:, c * col_chunk:(c + 1) * col_chunk])
        z_ref[:, c * z_chunk:(c + 1) * z_chunk] = _dot(n_ref[...], wz_ref[:, c * z_chunk:(c + 1) * z_chunk])

    project(0)
    for c in range(n_chunks):
        if c + 1 < n_chunks:
            project(c + 1)
        for c0 in range(c * col_chunk, (c + 1) * col_chunk, LANES):
            cs = slice(c0, c0 + LANES)
            w = cw_ref[:, cs]
            acc = cb_ref[:, cs]
            for k in range(SSD_CONV_WIDTH):
                shifted = u_ref[SUBLANES - k:SUBLANES - k + tm, cs]
                acc = acc + shifted * w[SSD_CONV_WIDTH - 1 - k:SSD_CONV_WIDTH - k]
            u_ref[:SUBLANES, cs] = u_ref[tm:tm + SUBLANES, cs]
            y = acc * jax.nn.sigmoid(acc)
            if c0 < d_inner:
                xs_ref[:, cs] = y
            elif c0 < d_inner + bc_width:
                b_ref[:, c0 - d_inner:c0 - d_inner + LANES] = y.astype(BF16)
            else:
                c_ref[:, c0 - d_inner - bc_width:c0 - d_inner - bc_width + LANES] = y.astype(BF16)


def _ssd_in(h, norm_w, wz, wx, wdt, conv_w, conv_b, dt_bias, *, tm=256, col_chunk=512):
    B, S, D = h.shape
    d_inner = wz.shape[1]
    conv_dim = wx.shape[1]
    bc_width = (conv_dim - d_inner) // 2
    tm = min(tm, S)
    tok = lambda b, i: (b, i, 0)
    return pl.pallas_call(
        functools.partial(_ssd_in_kernel, d_inner=d_inner, bc_width=bc_width, col_chunk=col_chunk),
        grid=(B, S // tm),
        in_specs=[
            pl.BlockSpec((None, tm, D), tok),
            _resident((1, D)),
            _resident((D, d_inner)),
            _resident((D, conv_dim)),
            _resident((D, LANES)),
            _resident((SSD_CONV_WIDTH, conv_dim)),
            _resident((1, conv_dim)),
            _resident((1, LANES)),
        ],
        out_specs=[
            pl.BlockSpec((None, tm, d_inner), tok),
            pl.BlockSpec((None, tm, d_inner), tok),
            pl.BlockSpec((None, tm, bc_width), tok),
            pl.BlockSpec((None, tm, bc_width), tok),
            pl.BlockSpec((None, tm, LANES), tok),
        ],
        out_shape=[
            jax.ShapeDtypeStruct((B, S, d_inner), F32),
            jax.ShapeDtypeStruct((B, S, d_inner), F32),
            jax.ShapeDtypeStruct((B, S, bc_width), BF16),
            jax.ShapeDtypeStruct((B, S, bc_width), BF16),
            jax.ShapeDtypeStruct((B, S, LANES), F32),
        ],
        scratch_shapes=[pltpu.VMEM((SUBLANES + tm, conv_dim), F32), pltpu.VMEM((tm, D), BF16)],
        compiler_params=_params("arbitrary", "arbitrary"),
        name="ssd_in",
    )(h, norm_w, wz, wx, wdt, conv_w, conv_b, dt_bias)


def _ssd_core_kernel(xs_ref, b_ref, c_ref, dt_ref, z_ref, alog_ref, dskip_ref, gn_ref, expand_ref,
                     y_ref, state_ref, *, n_groups, heads_per_group):
    L = xs_ref.shape[0]
    gw = heads_per_group * SSD_HEAD_DIM

    @pl.when(pl.program_id(1) == 0)
    def _():
        state_ref[...] = jnp.zeros_like(state_ref)

    row = lax.broadcasted_iota(jnp.int32, (L, L), 0)
    col = lax.broadcasted_iota(jnp.int32, (L, L), 1)
    causal = row >= col
    tril = causal.astype(BF16)
    expand = expand_ref[...]

    dt = dt_ref[...]
    a = dt * -jnp.exp(alog_ref[...])
    a_cs = sum(_dot(tril, part) for part in _split3(a))
    a_cs_t = a_cs.T
    dt_full = sum(_dot(part, expand) for part in _split3(dt)[:2])
    acs_full = sum(_dot(part, expand) for part in _split3(a_cs)[:2])
    a_end = acs_full[L - 1:L, :]

    x = xs_ref[...]
    x_dt = x * dt_full
    x_end = (x_dt * jnp.exp(a_end - acs_full)).astype(BF16)
    carry_in = jnp.exp(acs_full)
    chunk_decay = jnp.exp(a_end)
    head_of_lane = lax.broadcasted_iota(jnp.int32, (L, gw), 1) // SSD_HEAD_DIM

    for g in range(n_groups):
        gs = slice(g * gw, (g + 1) * gw)
        ns = slice(g * SSD_D_STATE, (g + 1) * SSD_D_STATE)
        b_g = b_ref[:, ns]
        c_g = c_ref[:, ns]
        cb = _dot_nt(c_g, b_g)
        state = state_ref[:, gs]
        y = _dot(c_g, state.astype(BF16)) * carry_in[:, gs]
        x_dt_g = x_dt[:, gs]
        mixes, inputs = [], []
        for r in range(heads_per_group):
            hh = g * heads_per_group + r
            seg = a_cs[:, hh:hh + 1] - a_cs_t[hh:hh + 1, :]
            decay = jnp.exp(jnp.where(causal, seg, -jnp.inf))
            mixes.append((cb * decay).astype(BF16))
            inputs.append(jnp.where(head_of_lane == r, x_dt_g, 0.0).astype(BF16))
        y = y + _dot(jnp.concatenate(mixes, axis=1), jnp.concatenate(inputs, axis=0))
        b_t = b_g.astype(F32).T.astype(BF16)
        state_ref[:, gs] = state * chunk_decay[:, gs] + _dot(b_t, x_end[:, gs])

        y = y + dskip_ref[:, gs] * x[:, gs]
        zg = z_ref[:, gs]
        y = y * (zg * jax.nn.sigmoid(zg))
        y = y * lax.rsqrt(jnp.mean(y * y, axis=-1, keepdims=True) + EPS) * gn_ref[:, gs]
        y_ref[:, gs] = y.astype(BF16)


def _ssd_core(xs, bm, cm, dt, z, a_log, d_skip, gnorm, expand):
    B, S, d_inner = xs.shape
    n_groups = SSD_N_GROUPS
    heads_per_group = d_inner // SSD_HEAD_DIM // n_groups
    bc_width = bm.shape[-1]
    L = SSD_CHUNK
    tok = lambda b, i: (b, i, 0)
    return pl.pallas_call(
        functools.partial(_ssd_core_kernel, n_groups=n_groups, heads_per_group=heads_per_group),
        grid=(B, S // L),
        in_specs=[
            pl.BlockSpec((None, L, d_inner), tok),
            pl.BlockSpec((None, L, bc_width), tok),
            pl.BlockSpec((None, L, bc_width), tok),
            pl.BlockSpec((None, L, LANES), tok),
            pl.BlockSpec((None, L, d_inner), tok),
            _resident((1, LANES)),
            _resident((1, d_inner)),
            _resident((1, d_inner)),
            _resident((LANES, d_inner)),
        ],
        out_specs=pl.BlockSpec((None, L, d_inner), tok),
        out_shape=jax.ShapeDtypeStruct((B, S, d_inner), BF16),
        scratch_shapes=[pltpu.VMEM((SSD_D_STATE, d_inner), F32)],
        compiler_params=_params("arbitrary", "arbitrary"),
        name="ssd_core",
    )(xs, bm, cm, dt, z, a_log, d_skip, gnorm, expand)


def kernel(x, mem, positions, ffn1_norm, ffn1_w_gate, ffn1_w_up, ffn1_w_down, mix_norm, da_w_in, da_lambda_q1, da_lambda_k1, da_lambda_q2, da_lambda_k2, da_subln, da_w_out, ssd_w_in, ssd_conv_w, ssd_conv_b, ssd_dt_bias, ssd_A_log, ssd_D, ssd_gnorm, ssd_w_out, xa_norm, xa_mem_norm, xa_w_q, xa_w_kv, xa_w_o, ffn2_norm, ffn2_w_gate, ffn2_w_up, ffn2_w_down, final_norm):
    B, S, D = x.shape
    depth = ffn1_norm.shape[0]
    T = B * S
    row = lambda v: v.reshape(1, -1).astype(F32)
    bf = lambda w: w.astype(BF16)

    inv = 1.0 / (ROPE_THETA ** (jnp.arange(0, DA_HEAD_DIM, 2, dtype=F32) / DA_HEAD_DIM))
    invf = jnp.tile(inv, LANES // inv.shape[0]).reshape(1, LANES)
    lane = jnp.arange(LANES)
    sign = jnp.where((lane % DA_HEAD_DIM) < DA_HEAD_DIM // 2, -1.0, 1.0).astype(F32).reshape(1, LANES)
    pos = positions.astype(F32).reshape(B, 1, S)

    mem_k, mem_v = _mem_kv(mem, xa_mem_norm.reshape(depth, 1, D), bf(xa_w_kv))

    h = x
    for i in range(depth):
        h = _ffn(h.reshape(T, D), row(ffn1_norm[i]), bf(ffn1_w_gate[i]), bf(ffn1_w_up[i]),
                 bf(ffn1_w_down[i]), row(final_norm), final=False, tm=1024).reshape(B, S, D)
        j = i // N_MIXERS
        if i % N_MIXERS == 0:
            q, k, v = _da_qkv(h, pos, row(mix_norm[i]), bf(da_w_in[j]), invf, sign)
            lam_vecs = jnp.stack([da_lambda_q1[j], da_lambda_k1[j], da_lambda_q2[j], da_lambda_k2[j]]).astype(F32)
            mixed = _da_attn(q, k, v, lam_vecs, da_subln[j].astype(F32).reshape(LANES, 1),
                             lambda_init=_lambda_init(i))
            w_mix_out = bf(da_w_out[j])
        else:
            d_inner = ssd_w_out.shape[1]
            n_heads = ssd_A_log.shape[1]
            conv_dim = ssd_conv_w.shape[2]
            w_in = ssd_w_in[j]
            wz = bf(w_in[:, :d_inner])
            wx = bf(w_in[:, d_inner:d_inner + conv_dim])
            wdt = bf(jnp.pad(w_in[:, d_inner + conv_dim:], ((0, 0), (0, LANES - n_heads))))
            dt_bias = jnp.pad(ssd_dt_bias[j].astype(F32), (0, LANES - n_heads)).reshape(1, LANES)
            z, xs, bm, cm, dt = _ssd_in(h, row(mix_norm[i]), wz, wx, wdt, ssd_conv_w[j].astype(F32),
                                        row(ssd_conv_b[j]), dt_bias, tm=512)
            a_log = jnp.pad(ssd_A_log[j].astype(F32), (0, LANES - n_heads)).reshape(1, LANES)
            d_skip = jnp.repeat(ssd_D[j].astype(F32), SSD_HEAD_DIM).reshape(1, d_inner)
            expand = (jnp.arange(LANES)[:, None] == (jnp.arange(d_inner) // SSD_HEAD_DIM)[None, :]).astype(BF16)
            mixed = _ssd_core(xs, bm, cm, dt, z, a_log, d_skip, row(ssd_gnorm[j]), expand)
            w_mix_out = bf(ssd_w_out[j])
        h = _xattn(h, mixed, w_mix_out, row(xa_norm[i]), bf(xa_w_q[i]), mem_k[i], mem_v[i], bf(xa_w_o[i]),
                   tm=512 * (i + 1))
        h = _ffn(h.reshape(T, D), row(ffn2_norm[i]), bf(ffn2_w_gate[i]), bf(ffn2_w_up[i]),
                 bf(ffn2_w_down[i]), row(final_norm), final=(i == depth - 1)).reshape(B, S, D)
    return h
```

```python
import functools
import math

import jax
import jax.numpy as jnp
from jax import lax
from jax.experimental import pallas as pl
from jax.experimental.pallas import tpu as pltpu

F32 = jnp.float32
BF16 = jnp.bfloat16

EPS = 1e-6
ROPE_THETA = 10000.0
DA_HEAD_DIM = 64
DA_Q_TILE = 512
SSD_HEAD_DIM = 64
SSD_N_GROUPS = 8
SSD_D_STATE = 128
SSD_CONV_WIDTH = 4
SSD_CHUNK = 128
XA_N_HEADS = 4
N_MIXERS = 2

LANES = 128
SUBLANES = 8
BF16_ROWS = 16
MXU_DIM = 256
VMEM_LIMIT = 56 * 1024 * 1024


def _lambda_init(layer_idx):
    return 0.8 - 0.6 * math.exp(-0.3 * layer_idx)


def _params(*semantics):
    return pltpu.CompilerParams(dimension_semantics=semantics, vmem_limit_bytes=VMEM_LIMIT)


def _resident(shape):
    zeros = (0,) * len(shape)
    return pl.BlockSpec(shape, lambda *_: zeros, pipeline_mode=pl.Buffered(1))


def _rmsnorm(x, w):
    return x * lax.rsqrt(jnp.mean(x * x, axis=-1, keepdims=True) + EPS) * w


def _dot(a, b):
    return jnp.dot(a, b, preferred_element_type=F32)


def _dot_nt(a, b):
    return lax.dot_general(a, b, (((1,), (1,)), ((), ())), preferred_element_type=F32)


def _split3(x):
    hi = x.astype(BF16)
    r1 = x - hi.astype(F32)
    mid = r1.astype(BF16)
    lo = (r1 - mid.astype(F32)).astype(BF16)
    return hi, mid, lo


def _ffn_kernel(h_ref, nw_ref, wg_ref, wu_ref, wd_ref, fw_ref, o_ref, *, f_chunks, final):
    h = h_ref[...]
    n = _rmsnorm(h, nw_ref[...]).astype(BF16)
    acc = None
    for c0, c1 in f_chunks:
        g = _dot(n, wg_ref[:, c0:c1])
        u = _dot(n, wu_ref[:, c0:c1])
        a = (g * jax.nn.sigmoid(g) * u).astype(BF16)
        d = _dot(a, wd_ref[c0:c1, :])
        acc = d if acc is None else acc + d
    out = h + 0.5 * acc
    if final:
        out = _rmsnorm(out, fw_ref[...])
    o_ref[...] = out


def _ffn(h, norm_w, wg, wu, wd, final_w, *, final, tm=1024, fc=1024):
    T, D = h.shape
    F = wg.shape[1]
    tm = min(tm, T)
    f_chunks = tuple((c, min(c + fc, F)) for c in range(0, F, fc))
    return pl.pallas_call(
        functools.partial(_ffn_kernel, f_chunks=f_chunks, final=final),
        grid=(T // tm,),
        in_specs=[
            pl.BlockSpec((tm, D), lambda i: (i, 0)),
            _resident((1, D)),
            _resident((D, F)),
            _resident((D, F)),
            _resident((F, D)),
            _resident((1, D)),
        ],
        out_specs=pl.BlockSpec((tm, D), lambda i: (i, 0)),
        out_shape=jax.ShapeDtypeStruct((T, D), F32),
        compiler_params=_params("arbitrary"),
        name="ffn_final" if final else "ffn",
    )(h, norm_w, wg, wu, wd, final_w)


def _da_qkv_kernel(h_ref, pos_ref, nw_ref, w_ref, invf_ref, sign_ref, q_ref, k_ref, v_ref, *, width):
    tm = h_ref.shape[0]
    n = _rmsnorm(h_ref[...], nw_ref[...]).astype(BF16)
    pos = jnp.broadcast_to(pos_ref[...], (LANES, tm)).T
    ang = pos * invf_ref[...]
    cos = jnp.cos(ang)
    sin = jnp.sin(ang) * sign_ref[...]
    lane = lax.broadcasted_iota(jnp.int32, (tm, LANES), 1)
    low_half = (lane % DA_HEAD_DIM) < (DA_HEAD_DIM // 2)
    half = DA_HEAD_DIM // 2

    def rope(x):
        partner = jnp.where(low_half, pltpu.roll(x, LANES - half, axis=1), pltpu.roll(x, half, axis=1))
        return x * cos + partner * sin

    q_scale = DA_HEAD_DIM ** -0.5 * math.log2(math.e)
    q = _dot(n, w_ref[:, :width])
    k = _dot(n, w_ref[:, width:2 * width])
    feature = lax.broadcasted_iota(jnp.int32, (LANES, tm), 0)
    for c in range(width // LANES):
        sl = slice(c * LANES, (c + 1) * LANES)
        qt = (rope(q[:, sl]) * q_scale).T
        q_ref[c, :, :tm] = jnp.where(feature < DA_HEAD_DIM, qt, 0.0).astype(BF16)
        q_ref[c, :, tm:] = jnp.where(feature >= DA_HEAD_DIM, qt, 0.0).astype(BF16)
        k_ref[:, sl] = rope(k[:, sl]).astype(BF16)
    v_ref[...] = _dot(n, w_ref[:, 2 * width:3 * width]).astype(BF16)


def _da_qkv(h, pos, norm_w, w_in, invf, sign, *, tm):
    B, S, D = h.shape
    width = w_in.shape[1] // 3
    n_heads = width // LANES
    tok = lambda b, i: (b, i, 0)
    out = jax.ShapeDtypeStruct((B, S, width), BF16)
    q_out = jax.ShapeDtypeStruct((B, n_heads, LANES, 2 * S), BF16)
    return pl.pallas_call(
        functools.partial(_da_qkv_kernel, width=width),
        grid=(B, S // tm),
        in_specs=[
            pl.BlockSpec((None, tm, D), tok),
            pl.BlockSpec((None, 1, tm), lambda b, i: (b, 0, i)),
            _resident((1, D)),
            _resident((D, 3 * width)),
            _resident((1, LANES)),
            _resident((1, LANES)),
        ],
        out_specs=[pl.BlockSpec((None, n_heads, LANES, 2 * tm), lambda b, i: (b, 0, 0, i)),
                   pl.BlockSpec((None, tm, width), tok), pl.BlockSpec((None, tm, width), tok)],
        out_shape=[q_out, out, out],
        compiler_params=_params("arbitrary", "arbitrary"),
        name="da_qkv",
    )(h, pos, norm_w, w_in, invf, sign)


def _transpose_bf16(x):
    return x.astype(F32).T.astype(BF16)


def _da_attn_kernel(q_ref, k_ref, v_ref, lam_ref, subln_ref, o_ref, *scratch, tq, heads, lambda_init):
    qi = pl.program_id(2)
    S = k_ref.shape[0]
    head_lanes = [slice(hh * LANES, (hh + 1) * LANES) for hh in range(heads)]
    vt_refs, s_refs, p_refs, acc_refs, st_refs = (scratch[i * heads:(i + 1) * heads] for i in range(5))

    @pl.when(qi == 0)
    def _():
        for hh, hl in enumerate(head_lanes):
            vt_refs[hh][LANES:, :] = jnp.ones((BF16_ROWS, S), BF16)
            for c in range(0, S, tq):
                vt_refs[hh][:LANES, c:c + tq] = _transpose_bf16(v_ref[c:c + tq, hl])

    for hh in range(heads):
        acc_refs[hh][...] = jnp.zeros_like(acc_refs[hh])
        st_refs[hh][0:1, :] = jnp.full((1, 2 * tq), -jnp.inf, F32)

    key = lax.broadcasted_iota(jnp.int32, (tq, LANES), 0)
    lane = lax.broadcasted_iota(jnp.int32, (tq, LANES), 1)

    def scores(hh, j):
        start = pl.multiple_of(j * tq, tq)
        s = _dot(k_ref[pl.ds(start, tq), head_lanes[hh]], q_ref[hh])
        s_refs[hh][...] = s
        st_refs[hh][2:3, :] = jnp.max(s, axis=0, keepdims=True)

    def softmax(hh, diagonal):
        s_ref, p_ref, st_ref = s_refs[hh], p_refs[hh], st_refs[hh]
        for c in range(0, 2 * tq, LANES):
            cs = slice(c, c + LANES)

            def strip():
                s = s_ref[:, cs]
                return jnp.where(key <= lane + (c % tq), s, -jnp.inf) if diagonal else s

            m_old = st_ref[0:1, cs]
            m_blk = jnp.max(strip(), axis=0, keepdims=True) if diagonal else st_ref[2:3, cs]
            m_new = jnp.maximum(m_old, m_blk)
            st_ref[0:1, cs] = m_new
            st_ref[1:2, cs] = jnp.exp2(m_old - m_new)
            p_ref[:, cs] = jnp.exp2(strip() - m_new).astype(BF16)

    def weighted_values(hh, j):
        start = pl.multiple_of(j * tq, tq)
        pv = _dot(vt_refs[hh][:, pl.ds(start, tq)], p_refs[hh][...])
        acc_refs[hh][...] = st_refs[hh][1:2, :] * acc_refs[hh][...] + pv

    def step(j, diagonal):
        for hh in range(heads):
            if hh + 1 < heads:
                scores(hh + 1, j)
            elif not diagonal:
                scores(0, j + 1)
            softmax(hh, diagonal)
            weighted_values(hh, j)

    def body(j, carry):
        step(j, False)
        return carry

    scores(0, 0)
    lax.fori_loop(0, qi, body, 0)
    step(qi, True)

    lam_v = lam_ref[...]
    lam = (jnp.exp(jnp.sum(lam_v[0:1] * lam_v[1:2], axis=-1, keepdims=True))
           - jnp.exp(jnp.sum(lam_v[2:3] * lam_v[3:4], axis=-1, keepdims=True))
           + lambda_init)
    for hh, hl in enumerate(head_lanes):
        o = acc_refs[hh][:LANES, :] / acc_refs[hh][LANES:LANES + 1, :]
        d = o[:, :tq] - lam * o[:, tq:]
        y = d * lax.rsqrt(jnp.mean(d * d, axis=0, keepdims=True) + EPS) * subln_ref[...]
        o_ref[:, hl] = (y * (1.0 - lambda_init)).T.astype(BF16)


def _da_attn(q, k, v, lam_vecs, subln_col, *, lambda_init, tq, heads=2):
    B, S, W = k.shape
    H = W // LANES
    hw = heads * LANES
    per_head = lambda shape, dtype: [pltpu.VMEM(shape, dtype) for _ in range(heads)]
    return pl.pallas_call(
        functools.partial(_da_attn_kernel, tq=tq, heads=heads, lambda_init=lambda_init),
        grid=(B, H // heads, S // tq),
        in_specs=[
            pl.BlockSpec((None, heads, LANES, 2 * tq), lambda b, h, i: (b, h, 0, i)),
            pl.BlockSpec((None, S, hw), lambda b, h, i: (b, 0, h)),
            pl.BlockSpec((None, S, hw), lambda b, h, i: (b, 0, h)),
            _resident(lam_vecs.shape),
            _resident((LANES, 1)),
        ],
        out_specs=pl.BlockSpec((None, tq, hw), lambda b, h, i: (b, i, h)),
        out_shape=jax.ShapeDtypeStruct((B, S, W), BF16),
        scratch_shapes=(per_head((LANES + BF16_ROWS, S), BF16) + per_head((tq, 2 * tq), F32)
                        + per_head((tq, 2 * tq), BF16) + per_head((LANES + BF16_ROWS, 2 * tq), F32)
                        + per_head((SUBLANES, 2 * tq), F32)),
        compiler_params=_params("arbitrary", "arbitrary", "arbitrary"),
        name="da_attn",
    )(q, k, v, lam_vecs, subln_col)


def _mem_kv_kernel(mem_ref, nw_ref, w_ref, k_ref, v_ref):
    D = mem_ref.shape[-1]
    m = _rmsnorm(mem_ref[...], nw_ref[...]).astype(BF16)
    kv = _dot(m, w_ref[...])
    k_ref[...] = kv[:, :D].astype(BF16)
    v_ref[...] = kv[:, D:].astype(BF16)


def _mem_kv(mem, norm_w, w_kv):
    B, M, D = mem.shape
    L = w_kv.shape[0]
    out = jax.ShapeDtypeStruct((L, B, M, D), BF16)
    return pl.pallas_call(
        _mem_kv_kernel,
        grid=(L, B),
        in_specs=[
            pl.BlockSpec((None, M, D), lambda l, b: (b, 0, 0)),
            pl.BlockSpec((None, 1, D), lambda l, b: (l, 0, 0)),
            pl.BlockSpec((None, D, 2 * D), lambda l, b: (l, 0, 0)),
        ],
        out_specs=[pl.BlockSpec((None, None, M, D), lambda l, b: (l, b, 0, 0))] * 2,
        out_shape=[out, out],
        compiler_params=_params("arbitrary", "arbitrary"),
        name="mem_kv",
    )(mem, norm_w, w_kv)


def _xattn_kernel(h_ref, a_ref, wpre_ref, nw_ref, wq_ref, k_ref, v_ref, wo_ref, o_ref, *, n_heads):
    h = h_ref[...] + _dot(a_ref[...], wpre_ref[...])
    D = h.shape[-1]
    hd = D // n_heads
    n = _rmsnorm(h, nw_ref[...]).astype(BF16)
    q = (_dot(n, wq_ref[...]) * (hd ** -0.5)).astype(BF16)
    heads = []
    for i in range(n_heads):
        sl = slice(i * hd, (i + 1) * hd)
        s = _dot_nt(q[:, sl], k_ref[:, sl])
        p = jnp.exp(s - jnp.max(s, axis=-1, keepdims=True))
        l = jnp.sum(p, axis=-1, keepdims=True)
        heads.append((_dot(p.astype(BF16), v_ref[:, sl]) / l).astype(BF16))
    o = jnp.concatenate(heads, axis=-1)
    o_ref[...] = h + _dot(o, wo_ref[...])


def _xattn(h, a, w_pre, norm_w, wq, k, v, wo, *, tm=1024):
    B, S, D = h.shape
    Ka = a.shape[-1]
    M = k.shape[1]
    tm = min(tm, S)
    tok = lambda b, i: (b, i, 0)
    return pl.pallas_call(
        functools.partial(_xattn_kernel, n_heads=XA_N_HEADS),
        grid=(B, S // tm),
        in_specs=[
            pl.BlockSpec((None, tm, D), tok),
            pl.BlockSpec((None, tm, Ka), tok),
            _resident((Ka, D)),
            _resident((1, D)),
            _resident((D, D)),
            pl.BlockSpec((None, M, D), lambda b, i: (b, 0, 0)),
            pl.BlockSpec((None, M, D), lambda b, i: (b, 0, 0)),
            _resident((D, D)),
        ],
        out_specs=pl.BlockSpec((None, tm, D), tok),
        out_shape=jax.ShapeDtypeStruct((B, S, D), F32),
        compiler_params=_params("arbitrary", "arbitrary"),
        name="xattn",
    )(h, a, w_pre, norm_w, wq, k, v, wo)


def _softplus(x):
    return jnp.maximum(x, 0.0) + jnp.log(1.0 + jnp.exp(-jnp.abs(x)))


def _ssd_in_kernel(h_ref, nw_ref, wz_ref, wx_ref, wdt_ref, cw_ref, cb_ref, dtb_ref,
                   z_ref, xs_ref, b_ref, c_ref, dt_ref, u_ref, n_ref, *, d_inner, bc_width, col_chunk):
    tm = h_ref.shape[0]

    @pl.when(pl.program_id(1) == 0)
    def _():
        u_ref[:SUBLANES, :] = jnp.zeros((SUBLANES, u_ref.shape[1]), F32)

    n_ref[...] = _rmsnorm(h_ref[...], nw_ref[...]).astype(BF16)
    dt_ref[...] = _softplus(_dot(n_ref[...], wdt_ref[...]) + dtb_ref[...])

    conv_dim = wx_ref.shape[1]
    n_chunks = conv_dim // col_chunk
    z_chunk = d_inner // n_chunks

    def project(c):
        u_ref[SUBLANES:, c * col_chunk:(c + 1) * col_chunk] = _dot(
            n_ref[...], wx_ref[:, c * col_chunk:(c + 1) * col_chunk])
        z_ref[:, c * z_chunk:(c + 1) * z_chunk] = _dot(n_ref[...], wz_ref[:, c * z_chunk:(c + 1) * z_chunk])

    project(0)
    for c in range(n_chunks):
        if c + 1 < n_chunks:
            project(c + 1)
        for c0 in range(c * col_chunk, (c + 1) * col_chunk, LANES):
            cs = slice(c0, c0 + LANES)
            w = cw_ref[:, cs]
            acc = cb_ref[:, cs]
            for k in range(SSD_CONV_WIDTH):
                shifted = u_ref[SUBLANES - k:SUBLANES - k + tm, cs]
                acc = acc + shifted * w[SSD_CONV_WIDTH - 1 - k:SSD_CONV_WIDTH - k]
            u_ref[:SUBLANES, cs] = u_ref[tm:tm + SUBLANES, cs]
            y = acc * jax.nn.sigmoid(acc)
            if c0 < d_inner:
                xs_ref[:, cs] = y
            elif c0 < d_inner + bc_width:
                b_ref[:, c0 - d_inner:c0 - d_inner + LANES] = y.astype(BF16)
            else:
                c_ref[:, c0 - d_inner - bc_width:c0 - d_inner - bc_width + LANES] = y.astype(BF16)


def _ssd_in(h, norm_w, wz, wx, wdt, conv_w, conv_b, dt_bias, *, tm=512, col_chunk=512):
    B, S, D = h.shape
    d_inner = wz.shape[1]
    conv_dim = wx.shape[1]
    bc_width = (conv_dim - d_inner) // 2
    tm = min(tm, S)
    tok = lambda b, i: (b, i, 0)
    return pl.pallas_call(
        functools.partial(_ssd_in_kernel, d_inner=d_inner, bc_width=bc_width, col_chunk=col_chunk),
        grid=(B, S // tm),
        in_specs=[
            pl.BlockSpec((None, tm, D), tok),
            _resident((1, D)),
            _resident((D, d_inner)),
            _resident((D, conv_dim)),
            _resident((D, LANES)),
            _resident((SSD_CONV_WIDTH, conv_dim)),
            _resident((1, conv_dim)),
            _resident((1, LANES)),
        ],
        out_specs=[
            pl.BlockSpec((None, tm, d_inner), tok),
            pl.BlockSpec((None, tm, d_inner), tok),
            pl.BlockSpec((None, tm, bc_width), tok),
            pl.BlockSpec((None, tm, bc_width), tok),
            pl.BlockSpec((None, tm, LANES), tok),
        ],
        out_shape=[
            jax.ShapeDtypeStruct((B, S, d_inner), F32),
            jax.ShapeDtypeStruct((B, S, d_inner), F32),
            jax.ShapeDtypeStruct((B, S, bc_width), BF16),
            jax.ShapeDtypeStruct((B, S, bc_width), BF16),
            jax.ShapeDtypeStruct((B, S, LANES), F32),
        ],
        scratch_shapes=[pltpu.VMEM((SUBLANES + tm, conv_dim), F32), pltpu.VMEM((tm, D), BF16)],
        compiler_params=_params("arbitrary", "arbitrary"),
        name="ssd_in",
    )(h, norm_w, wz, wx, wdt, conv_w, conv_b, dt_bias)


def _ssd_core_kernel(xs_ref, b_ref, c_ref, dt_ref, z_ref, alog_ref, dskip_ref, gn_ref, expand_ref,
                     y_ref, state_ref, *, n_groups, heads_per_group):
    L = xs_ref.shape[0]
    gw = heads_per_group * SSD_HEAD_DIM

    @pl.when(pl.program_id(1) == 0)
    def _():
        state_ref[...] = jnp.zeros_like(state_ref)

    row = lax.broadcasted_iota(jnp.int32, (L, L), 0)
    col = lax.broadcasted_iota(jnp.int32, (L, L), 1)
    causal = row >= col
    tril = causal.astype(BF16)
    expand = expand_ref[...]

    dt = dt_ref[...]
    a = dt * -jnp.exp(alog_ref[...])
    a_cs = sum(_dot(tril, part) for part in _split3(a))
    a_cs_t = a_cs.T
    dt_full = sum(_dot(part, expand) for part in _split3(dt)[:2])
    acs_full = sum(_dot(part, expand) for part in _split3(a_cs)[:2])
    a_end = acs_full[L - 1:L, :]

    x = xs_ref[...]
    x_dt = x * dt_full
    x_end = (x_dt * jnp.exp(a_end - acs_full)).astype(BF16)
    carry_in = jnp.exp(acs_full)
    chunk_decay = jnp.exp(a_end)
    head_of_lane = lax.broadcasted_iota(jnp.int32, (L, gw), 1) // SSD_HEAD_DIM

    for g in range(n_groups):
        gs = slice(g * gw, (g + 1) * gw)
        ns = slice(g * SSD_D_STATE, (g + 1) * SSD_D_STATE)
        b_g = b_ref[:, ns]
        c_g = c_ref[:, ns]
        cb = _dot_nt(c_g, b_g)
        state = state_ref[:, gs]
        y = _dot(c_g, state.astype(BF16)) * carry_in[:, gs]
        x_dt_g = x_dt[:, gs]
        mixes, inputs = [], []
        for r in range(heads_per_group):
            hh = g * heads_per_group + r
            seg = a_cs[:, hh:hh + 1] - a_cs_t[hh:hh + 1, :]
            decay = jnp.exp(jnp.where(causal, seg, -jnp.inf))
            mixes.append((cb * decay).astype(BF16))
            inputs.append(jnp.where(head_of_lane == r, x_dt_g, 0.0).astype(BF16))
        y = y + _dot(jnp.concatenate(mixes, axis=1), jnp.concatenate(inputs, axis=0))
        b_t = b_g.astype(F32).T.astype(BF16)
        state_ref[:, gs] = state * chunk_decay[:, gs] + _dot(b_t, x_end[:, gs])

        y = y + dskip_ref[:, gs] * x[:, gs]
        zg = z_ref[:, gs]
        y = y * (zg * jax.nn.sigmoid(zg))
        y = y * lax.rsqrt(jnp.mean(y * y, axis=-1, keepdims=True) + EPS) * gn_ref[:, gs]
        y_ref[:, gs] = y.astype(BF16)


def _ssd_core(xs, bm, cm, dt, z, a_log, d_skip, gnorm, expand):
    B, S, d_inner = xs.shape
    n_groups = SSD_N_GROUPS
    heads_per_group = d_inner // SSD_HEAD_DIM // n_groups
    bc_width = bm.shape[-1]
    L = SSD_CHUNK
    tok = lambda b, i: (b, i, 0)
    return pl.pallas_call(
        functools.partial(_ssd_core_kernel, n_groups=n_groups, heads_per_group=heads_per_group),
        grid=(B, S // L),
        in_specs=[
            pl.BlockSpec((None, L, d_inner), tok),
            pl.BlockSpec((None, L, bc_width), tok),
            pl.BlockSpec((None, L, bc_width), tok),
            pl.BlockSpec((None, L, LANES), tok),
            pl.BlockSpec((None, L, d_inner), tok),
            _resident((1, LANES)),
            _resident((1, d_inner)),
            _resident((1, d_inner)),
            _resident((LANES, d_inner)),
        ],
        out_specs=pl.BlockSpec((None, L, d_inner), tok),
        out_shape=jax.ShapeDtypeStruct((B, S, d_inner), BF16),
        scratch_shapes=[pltpu.VMEM((SSD_D_STATE, d_inner), F32)],
        compiler_params=_params("arbitrary", "arbitrary"),
        name="ssd_core",
    )(xs, bm, cm, dt, z, a_log, d_skip, gnorm, expand)


def kernel(x, mem, positions, ffn1_norm, ffn1_w_gate, ffn1_w_up, ffn1_w_down, mix_norm, da_w_in, da_lambda_q1, da_lambda_k1, da_lambda_q2, da_lambda_k2, da_subln, da_w_out, ssd_w_in, ssd_conv_w, ssd_conv_b, ssd_dt_bias, ssd_A_log, ssd_D, ssd_gnorm, ssd_w_out, xa_norm, xa_mem_norm, xa_w_q, xa_w_kv, xa_w_o, ffn2_norm, ffn2_w_gate, ffn2_w_up, ffn2_w_down, final_norm):
    B, S, D = x.shape
    depth = ffn1_norm.shape[0]
    T = B * S
    row = lambda v: v.reshape(1, -1).astype(F32)
    bf = lambda w: w.astype(BF16)

    inv = 1.0 / (ROPE_THETA ** (jnp.arange(0, DA_HEAD_DIM, 2, dtype=F32) / DA_HEAD_DIM))
    invf = jnp.tile(inv, LANES // inv.shape[0]).reshape(1, LANES)
    lane = jnp.arange(LANES)
    sign = jnp.where((lane % DA_HEAD_DIM) < DA_HEAD_DIM // 2, -1.0, 1.0).astype(F32).reshape(1, LANES)
    pos = positions.astype(F32).reshape(B, 1, S)

    mem_k, mem_v = _mem_kv(mem, xa_mem_norm.reshape(depth, 1, D), bf(xa_w_kv))

    h = x
    for i in range(depth):
        h = _ffn(h.reshape(T, D), row(ffn1_norm[i]), bf(ffn1_w_gate[i]), bf(ffn1_w_up[i]),
                 bf(ffn1_w_down[i]), row(final_norm), final=False).reshape(B, S, D)
        j = i // N_MIXERS
        if i % N_MIXERS == 0:
            tq = min(DA_Q_TILE, S)
            q, k, v = _da_qkv(h, pos, row(mix_norm[i]), bf(da_w_in[j]), invf, sign, tm=tq)
            lam_vecs = jnp.stack([da_lambda_q1[j], da_lambda_k1[j], da_lambda_q2[j], da_lambda_k2[j]]).astype(F32)
            mixed = _da_attn(q, k, v, lam_vecs, da_subln[j].astype(F32).reshape(LANES, 1),
                             lambda_init=_lambda_init(i), tq=tq)
            w_mix_out = bf(da_w_out[j])
        else:
            d_inner = ssd_w_out.shape[1]
            n_heads = ssd_A_log.shape[1]
            conv_dim = ssd_conv_w.shape[2]
            w_in = ssd_w_in[j]
            wz = bf(w_in[:, :d_inner])
            wx = bf(w_in[:, d_inner:d_inner + conv_dim])
            wdt = bf(jnp.pad(w_in[:, d_inner + conv_dim:], ((0, 0), (0, LANES - n_heads))))
            dt_bias = jnp.pad(ssd_dt_bias[j].astype(F32), (0, LANES - n_heads)).reshape(1, LANES)
            z, xs, bm, cm, dt = _ssd_in(h, row(mix_norm[i]), wz, wx, wdt, ssd_conv_w[j].astype(F32),
                                        row(ssd_conv_b[j]), dt_bias)
            a_log = jnp.pad(ssd_A_log[j].astype(F32), (0, LANES - n_heads)).reshape(1, LANES)
            d_skip = jnp.repeat(ssd_D[j].astype(F32), SSD_HEAD_DIM).reshape(1, d_inner)
            expand = (jnp.arange(LANES)[:, None] == (jnp.arange(d_inner) // SSD_HEAD_DIM)[None, :]).astype(BF16)
            mixed = _ssd_core(xs, bm, cm, dt, z, a_log, d_skip, row(ssd_gnorm[j]), expand)
            w_mix_out = bf(ssd_w_out[j])
        h = _xattn(h, mixed, w_mix_out, row(xa_norm[i]), bf(xa_w_q[i]), mem_k[i], mem_v[i], bf(xa_w_o[i]))
        h = _ffn(h.reshape(T, D), row(ffn2_norm[i]), bf(ffn2_w_gate[i]), bf(ffn2_w_up[i]),
                 bf(ffn2_w_down[i]), row(final_norm), final=(i == depth - 1)).reshape(B, S, D)
    return h
```

```python
import functools
import math

import jax
import jax.numpy as jnp
from jax import lax
from jax.experimental import pallas as pl
from jax.experimental.pallas import tpu as pltpu

F32 = jnp.float32
BF16 = jnp.bfloat16

EPS = 1e-6
ROPE_THETA = 10000.0
DA_HEAD_DIM = 64
DA_Q_TILE = 512
SSD_HEAD_DIM = 64
SSD_N_GROUPS = 8
SSD_D_STATE = 128
SSD_CONV_WIDTH = 4
SSD_CHUNK = 128
XA_N_HEADS = 4
N_MIXERS = 2

LANES = 128
SUBLANES = 8
BF16_ROWS = 16
MXU_DIM = 256
VMEM_LIMIT = 56 * 1024 * 1024


def _lambda_init(layer_idx):
    return 0.8 - 0.6 * math.exp(-0.3 * layer_idx)


def _params(*semantics):
    return pltpu.CompilerParams(dimension_semantics=semantics, vmem_limit_bytes=VMEM_LIMIT)


def _resident(shape):
    zeros = (0,) * len(shape)
    return pl.BlockSpec(shape, lambda *_: zeros, pipeline_mode=pl.Buffered(1))


def _rmsnorm(x, w):
    return x * lax.rsqrt(jnp.mean(x * x, axis=-1, keepdims=True) + EPS) * w


def _dot(a, b):
    return jnp.dot(a, b, preferred_element_type=F32)


def _dot_nt(a, b):
    return lax.dot_general(a, b, (((1,), (1,)), ((), ())), preferred_element_type=F32)


def _split3(x):
    hi = x.astype(BF16)
    r1 = x - hi.astype(F32)
    mid = r1.astype(BF16)
    lo = (r1 - mid.astype(F32)).astype(BF16)
    return hi, mid, lo


def _ffn_kernel(h_ref, nw_ref, wg_ref, wu_ref, wd_ref, fw_ref, o_ref, *, f_chunks, final):
    h = h_ref[...]
    n = _rmsnorm(h, nw_ref[...]).astype(BF16)
    acc = None
    for c0, c1 in f_chunks:
        g = _dot(n, wg_ref[:, c0:c1])
        u = _dot(n, wu_ref[:, c0:c1])
        a = (g * jax.nn.sigmoid(g) * u).astype(BF16)
        d = _dot(a, wd_ref[c0:c1, :])
        acc = d if acc is None else acc + d
    out = h + 0.5 * acc
    if final:
        out = _rmsnorm(out, fw_ref[...])
    o_ref[...] = out


def _ffn(h, norm_w, wg, wu, wd, final_w, *, final, tm=1024, fc=1024):
    T, D = h.shape
    F = wg.shape[1]
    tm = min(tm, T)
    f_chunks = tuple((c, min(c + fc, F)) for c in range(0, F, fc))
    return pl.pallas_call(
        functools.partial(_ffn_kernel, f_chunks=f_chunks, final=final),
        grid=(T // tm,),
        in_specs=[
            pl.BlockSpec((tm, D), lambda i: (i, 0)),
            _resident((1, D)),
            _resident((D, F)),
            _resident((D, F)),
            _resident((F, D)),
            _resident((1, D)),
        ],
        out_specs=pl.BlockSpec((tm, D), lambda i: (i, 0)),
        out_shape=jax.ShapeDtypeStruct((T, D), F32),
        compiler_params=_params("arbitrary"),
        name="ffn_final" if final else "ffn",
    )(h, norm_w, wg, wu, wd, final_w)


def _da_qkv_kernel(h_ref, pos_ref, nw_ref, w_ref, invf_ref, sign_ref, q_ref, k_ref, v_ref, *, width):
    tm = h_ref.shape[0]
    n = _rmsnorm(h_ref[...], nw_ref[...]).astype(BF16)
    pos = jnp.broadcast_to(pos_ref[...], (LANES, tm)).T
    ang = pos * invf_ref[...]
    cos = jnp.cos(ang)
    sin = jnp.sin(ang) * sign_ref[...]
    lane = lax.broadcasted_iota(jnp.int32, (tm, LANES), 1)
    low_half = (lane % DA_HEAD_DIM) < (DA_HEAD_DIM // 2)
    half = DA_HEAD_DIM // 2

    def rope(x):
        partner = jnp.where(low_half, pltpu.roll(x, LANES - half, axis=1), pltpu.roll(x, half, axis=1))
        return x * cos + partner * sin

    q_scale = DA_HEAD_DIM ** -0.5 * math.log2(math.e)
    q = _dot(n, w_ref[:, :width])
    k = _dot(n, w_ref[:, width:2 * width])
    feature = lax.broadcasted_iota(jnp.int32, (LANES, tm), 0)
    for c in range(width // LANES):
        sl = slice(c * LANES, (c + 1) * LANES)
        qt = (rope(q[:, sl]) * q_scale).T
        q_ref[c, :, :tm] = jnp.where(feature < DA_HEAD_DIM, qt, 0.0).astype(BF16)
        q_ref[c, :, tm:] = jnp.where(feature >= DA_HEAD_DIM, qt, 0.0).astype(BF16)
        k_ref[:, sl] = rope(k[:, sl]).astype(BF16)
    v_ref[...] = _dot(n, w_ref[:, 2 * width:3 * width]).astype(BF16)


def _da_qkv(h, pos, norm_w, w_in, invf, sign, *, tm):
    B, S, D = h.shape
    width = w_in.shape[1] // 3
    n_heads = width // LANES
    tok = lambda b, i: (b, i, 0)
    out = jax.ShapeDtypeStruct((B, S, width), BF16)
    q_out = jax.ShapeDtypeStruct((B, n_heads, LANES, 2 * S), BF16)
    return pl.pallas_call(
        functools.partial(_da_qkv_kernel, width=width),
        grid=(B, S // tm),
        in_specs=[
            pl.BlockSpec((None, tm, D), tok),
            pl.BlockSpec((None, 1, tm), lambda b, i: (b, 0, i)),
            _resident((1, D)),
            _resident((D, 3 * width)),
            _resident((1, LANES)),
            _resident((1, LANES)),
        ],
        out_specs=[pl.BlockSpec((None, n_heads, LANES, 2 * tm), lambda b, i: (b, 0, 0, i)),
                   pl.BlockSpec((None, tm, width), tok), pl.BlockSpec((None, tm, width), tok)],
        out_shape=[q_out, out, out],
        compiler_params=_params("arbitrary", "arbitrary"),
        name="da_qkv",
    )(h, pos, norm_w, w_in, invf, sign)


def _transpose_bf16(x):
    return x.astype(F32).T.astype(BF16)


def _da_attn_kernel(q_ref, k_ref, v_ref, lam_ref, subln_ref, o_ref, *scratch, tq, heads, lambda_init):
    qi = pl.program_id(2)
    S = k_ref.shape[0]
    head_lanes = [slice(hh * LANES, (hh + 1) * LANES) for hh in range(heads)]
    vt_refs, s_refs, p_refs, acc_refs, st_refs = (scratch[i * heads:(i + 1) * heads] for i in range(5))

    @pl.when(qi == 0)
    def _():
        for hh, hl in enumerate(head_lanes):
            vt_refs[hh][LANES:, :] = jnp.ones((BF16_ROWS, S), BF16)
            for c in range(0, S, tq):
                vt_refs[hh][:LANES, c:c + tq] = _transpose_bf16(v_ref[c:c + tq, hl])

    for hh in range(heads):
        acc_refs[hh][...] = jnp.zeros_like(acc_refs[hh])
        st_refs[hh][0:1, :] = jnp.full((1, 2 * tq), -jnp.inf, F32)

    key = lax.broadcasted_iota(jnp.int32, (LANES, LANES), 0)
    lane = lax.broadcasted_iota(jnp.int32, (LANES, LANES), 1)

    def scores(hh, j):
        start = pl.multiple_of(j * tq, tq)
        s = _dot(k_ref[pl.ds(start, tq), head_lanes[hh]], q_ref[hh])
        s_refs[hh][...] = s
        st_refs[hh][2:3, :] = jnp.max(s, axis=0, keepdims=True)

    def softmax(hh, diagonal):
        s_ref, p_ref, st_ref = s_refs[hh], p_refs[hh], st_refs[hh]
        for c in range(0, 2 * tq, LANES):
            cs = slice(c, c + LANES)
            m_old = st_ref[0:1, cs]
            if not diagonal:
                m_new = jnp.maximum(m_old, st_ref[2:3, cs])
                st_ref[0:1, cs] = m_new
                st_ref[1:2, cs] = jnp.exp2(m_old - m_new)
                p_ref[:, cs] = jnp.exp2(s_ref[:, cs] - m_new).astype(BF16)
                continue
            q0 = c % tq
            edge = slice(q0, q0 + LANES)

            def edge_scores():
                return jnp.where(key <= lane, s_ref[edge, cs], -jnp.inf)

            m_blk = jnp.max(edge_scores(), axis=0, keepdims=True)
            if q0 > 0:
                m_blk = jnp.maximum(m_blk, jnp.max(s_ref[:q0, cs], axis=0, keepdims=True))
            m_new = jnp.maximum(m_old, m_blk)
            st_ref[0:1, cs] = m_new
            st_ref[1:2, cs] = jnp.exp2(m_old - m_new)
            if q0 > 0:
                p_ref[:q0, cs] = jnp.exp2(s_ref[:q0, cs] - m_new).astype(BF16)
            p_ref[edge, cs] = jnp.exp2(edge_scores() - m_new).astype(BF16)
            if q0 + LANES < tq:
                p_ref[q0 + LANES:, cs] = jnp.zeros((tq - q0 - LANES, LANES), BF16)

    def weighted_values(hh, j):
        start = pl.multiple_of(j * tq, tq)
        pv = _dot(vt_refs[hh][:, pl.ds(start, tq)], p_refs[hh][...])
        acc_refs[hh][...] = st_refs[hh][1:2, :] * acc_refs[hh][...] + pv

    def step(j, diagonal):
        for hh in range(heads):
            if hh + 1 < heads:
                scores(hh + 1, j)
            elif not diagonal:
                scores(0, j + 1)
            softmax(hh, diagonal)
            weighted_values(hh, j)

    def body(j, carry):
        step(j, False)
        return carry

    scores(0, 0)
    lax.fori_loop(0, qi, body, 0)
    step(qi, True)

    lam_v = lam_ref[...]
    lam = (jnp.exp(jnp.sum(lam_v[0:1] * lam_v[1:2], axis=-1, keepdims=True))
           - jnp.exp(jnp.sum(lam_v[2:3] * lam_v[3:4], axis=-1, keepdims=True))
           + lambda_init)
    for hh, hl in enumerate(head_lanes):
        o = acc_refs[hh][:LANES, :] / acc_refs[hh][LANES:LANES + 1, :]
        d = o[:, :tq] - lam * o[:, tq:]
        y = d * lax.rsqrt(jnp.mean(d * d, axis=0, keepdims=True) + EPS) * subln_ref[...]
        o_ref[:, hl] = (y * (1.0 - lambda_init)).T.astype(BF16)


def _da_attn(q, k, v, lam_vecs, subln_col, *, lambda_init, tq, heads=2):
    B, S, W = k.shape
    H = W // LANES
    hw = heads * LANES
    per_head = lambda shape, dtype: [pltpu.VMEM(shape, dtype) for _ in range(heads)]
    return pl.pallas_call(
        functools.partial(_da_attn_kernel, tq=tq, heads=heads, lambda_init=lambda_init),
        grid=(B, H // heads, S // tq),
        in_specs=[
            pl.BlockSpec((None, heads, LANES, 2 * tq), lambda b, h, i: (b, h, 0, i)),
            pl.BlockSpec((None, S, hw), lambda b, h, i: (b, 0, h)),
            pl.BlockSpec((None, S, hw), lambda b, h, i: (b, 0, h)),
            _resident(lam_vecs.shape),
            _resident((LANES, 1)),
        ],
        out_specs=pl.BlockSpec((None, tq, hw), lambda b, h, i: (b, i, h)),
        out_shape=jax.ShapeDtypeStruct((B, S, W), BF16),
        scratch_shapes=(per_head((LANES + BF16_ROWS, S), BF16) + per_head((tq, 2 * tq), F32)
                        + per_head((tq, 2 * tq), BF16) + per_head((LANES + BF16_ROWS, 2 * tq), F32)
                        + per_head((SUBLANES, 2 * tq), F32)),
        compiler_params=_params("arbitrary", "arbitrary", "arbitrary"),
        name="da_attn",
    )(q, k, v, lam_vecs, subln_col)


def _mem_kv_kernel(mem_ref, nw_ref, w_ref, k_ref, v_ref):
    D = mem_ref.shape[-1]
    m = _rmsnorm(mem_ref[...], nw_ref[...]).astype(BF16)
    kv = _dot(m, w_ref[...])
    k_ref[...] = kv[:, :D].astype(BF16)
    v_ref[...] = kv[:, D:].astype(BF16)


def _mem_kv(mem, norm_w, w_kv):
    B, M, D = mem.shape
    L = w_kv.shape[0]
    out = jax.ShapeDtypeStruct((L, B, M, D), BF16)
    return pl.pallas_call(
        _mem_kv_kernel,
        grid=(L, B),
        in_specs=[
            pl.BlockSpec((None, M, D), lambda l, b: (b, 0, 0)),
            pl.BlockSpec((None, 1, D), lambda l, b: (l, 0, 0)),
            pl.BlockSpec((None, D, 2 * D), lambda l, b: (l, 0, 0)),
        ],
        out_specs=[pl.BlockSpec((None, None, M, D), lambda l, b: (l, b, 0, 0))] * 2,
        out_shape=[out, out],
        compiler_params=_params("arbitrary", "arbitrary"),
        name="mem_kv",
    )(mem, norm_w, w_kv)


def _xattn_kernel(h_ref, a_ref, wpre_ref, nw_ref, wq_ref, k_ref, v_ref, wo_ref, o_ref, *, n_heads):
    h = h_ref[...] + _dot(a_ref[...], wpre_ref[...])
    D = h.shape[-1]
    hd = D // n_heads
    n = _rmsnorm(h, nw_ref[...]).astype(BF16)
    q = (_dot(n, wq_ref[...]) * (hd ** -0.5)).astype(BF16)
    heads = []
    for i in range(n_heads):
        sl = slice(i * hd, (i + 1) * hd)
        s = _dot_nt(q[:, sl], k_ref[:, sl])
        p = jnp.exp(s - jnp.max(s, axis=-1, keepdims=True))
        l = jnp.sum(p, axis=-1, keepdims=True)
        heads.append((_dot(p.astype(BF16), v_ref[:, sl]) / l).astype(BF16))
    o = jnp.concatenate(heads, axis=-1)
    o_ref[...] = h + _dot(o, wo_ref[...])


def _xattn(h, a, w_pre, norm_w, wq, k, v, wo, *, tm=1024):
    B, S, D = h.shape
    Ka = a.shape[-1]
    M = k.shape[1]
    tm = min(tm, S)
    tok = lambda b, i: (b, i, 0)
    return pl.pallas_call(
        functools.partial(_xattn_kernel, n_heads=XA_N_HEADS),
        grid=(B, S // tm),
        in_specs=[
            pl.BlockSpec((None, tm, D), tok),
            pl.BlockSpec((None, tm, Ka), tok),
            _resident((Ka, D)),
            _resident((1, D)),
            _resident((D, D)),
            pl.BlockSpec((None, M, D), lambda b, i: (b, 0, 0)),
            pl.BlockSpec((None, M, D), lambda b, i: (b, 0, 0)),
            _resident((D, D)),
        ],
        out_specs=pl.BlockSpec((None, tm, D), tok),
        out_shape=jax.ShapeDtypeStruct((B, S, D), F32),
        compiler_params=_params("arbitrary", "arbitrary"),
        name="xattn",
    )(h, a, w_pre, norm_w, wq, k, v, wo)


def _softplus(x):
    return jnp.maximum(x, 0.0) + jnp.log(1.0 + jnp.exp(-jnp.abs(x)))


def _ssd_in_kernel(h_ref, nw_ref, w_ref, wdt_ref, cw_ref, cb_ref, dtb_ref,
                   z_ref, xs_ref, b_ref, c_ref, dt_ref, u_ref, n_ref, *, d_inner, bc_width, col_chunk):
    tm = h_ref.shape[0]

    @pl.when(pl.program_id(1) == 0)
    def _():
        u_ref[:SUBLANES, :] = jnp.zeros((SUBLANES, u_ref.shape[1]), F32)

    n_ref[...] = _rmsnorm(h_ref[...], nw_ref[...]).astype(BF16)
    dt_ref[...] = _softplus(_dot(n_ref[...], wdt_ref[...]) + dtb_ref[...])

    conv_dim = cw_ref.shape[1]
    n_chunks = conv_dim // col_chunk
    z_chunk = d_inner // n_chunks

    def project(c):
        u_ref[SUBLANES:, c * col_chunk:(c + 1) * col_chunk] = _dot(
            n_ref[...], w_ref[:, d_inner + c * col_chunk:d_inner + (c + 1) * col_chunk])
        z_ref[:, c * z_chunk:(c + 1) * z_chunk] = _dot(n_ref[...], w_ref[:, c * z_chunk:(c + 1) * z_chunk])

    project(0)
    for c in range(n_chunks):
        if c + 1 < n_chunks:
            project(c + 1)
        for c0 in range(c * col_chunk, (c + 1) * col_chunk, LANES):
            cs = slice(c0, c0 + LANES)
            w = cw_ref[:, cs]
            acc = cb_ref[:, cs]
            for k in range(SSD_CONV_WIDTH):
                shifted = u_ref[SUBLANES - k:SUBLANES - k + tm, cs]
                acc = acc + shifted * w[SSD_CONV_WIDTH - 1 - k:SSD_CONV_WIDTH - k]
            u_ref[:SUBLANES, cs] = u_ref[tm:tm + SUBLANES, cs]
            y = acc * jax.nn.sigmoid(acc)
            if c0 < d_inner:
                xs_ref[:, cs] = y
            elif c0 < d_inner + bc_width:
                b_ref[:, c0 - d_inner:c0 - d_inner + LANES] = y.astype(BF16)
            else:
                c_ref[:, c0 - d_inner - bc_width:c0 - d_inner - bc_width + LANES] = y.astype(BF16)


def _ssd_in(h, norm_w, w_in, wdt, conv_w, conv_b, dt_bias, *, d_inner, tm=512, col_chunk=512):
    B, S, D = h.shape
    conv_dim = conv_w.shape[1]
    bc_width = (conv_dim - d_inner) // 2
    tm = min(tm, S)
    tok = lambda b, i: (b, i, 0)
    return pl.pallas_call(
        functools.partial(_ssd_in_kernel, d_inner=d_inner, bc_width=bc_width, col_chunk=col_chunk),
        grid=(B, S // tm),
        in_specs=[
            pl.BlockSpec((None, tm, D), tok),
            _resident((1, D)),
            _resident(w_in.shape),
            _resident((D, LANES)),
            _resident((SSD_CONV_WIDTH, conv_dim)),
            _resident((1, conv_dim)),
            _resident((1, LANES)),
        ],
        out_specs=[
            pl.BlockSpec((None, tm, d_inner), tok),
            pl.BlockSpec((None, tm, d_inner), tok),
            pl.BlockSpec((None, tm, bc_width), tok),
            pl.BlockSpec((None, tm, bc_width), tok),
            pl.BlockSpec((None, tm, LANES), tok),
        ],
        out_shape=[
            jax.ShapeDtypeStruct((B, S, d_inner), F32),
            jax.ShapeDtypeStruct((B, S, d_inner), F32),
            jax.ShapeDtypeStruct((B, S, bc_width), BF16),
            jax.ShapeDtypeStruct((B, S, bc_width), BF16),
            jax.ShapeDtypeStruct((B, S, LANES), F32),
        ],
        scratch_shapes=[pltpu.VMEM((SUBLANES + tm, conv_dim), F32), pltpu.VMEM((tm, D), BF16)],
        compiler_params=_params("arbitrary", "arbitrary"),
        name="ssd_in",
    )(h, norm_w, w_in, wdt, conv_w, conv_b, dt_bias)


def _ssd_core_kernel(xs_ref, b_ref, c_ref, dt_ref, z_ref, alog_ref, dskip_ref, gn_ref, expand_ref,
                     y_ref, state_ref, *, n_groups, heads_per_group):
    L = xs_ref.shape[0]
    gw = heads_per_group * SSD_HEAD_DIM

    @pl.when(pl.program_id(1) == 0)
    def _():
        state_ref[...] = jnp.zeros_like(state_ref)

    row = lax.broadcasted_iota(jnp.int32, (L, L), 0)
    col = lax.broadcasted_iota(jnp.int32, (L, L), 1)
    causal = row >= col
    tril = causal.astype(BF16)
    expand = expand_ref[...]

    dt = dt_ref[...]
    a = dt * -jnp.exp(alog_ref[...])
    a_cs = sum(_dot(tril, part) for part in _split3(a))
    a_cs_t = a_cs.T
    dt_full = sum(_dot(part, expand) for part in _split3(dt)[:2])
    acs_full = sum(_dot(part, expand) for part in _split3(a_cs)[:2])
    a_end = acs_full[L - 1:L, :]

    x = xs_ref[...]
    x_dt = x * dt_full
    x_end = (x_dt * jnp.exp(a_end - acs_full)).astype(BF16)
    carry_in = jnp.exp(acs_full)
    chunk_decay = jnp.exp(a_end)
    head_of_lane = lax.broadcasted_iota(jnp.int32, (L, gw), 1) // SSD_HEAD_DIM

    for g in range(n_groups):
        gs = slice(g * gw, (g + 1) * gw)
        ns = slice(g * SSD_D_STATE, (g + 1) * SSD_D_STATE)
        b_g = b_ref[:, ns]
        c_g = c_ref[:, ns]
        cb = _dot_nt(c_g, b_g)
        state = state_ref[:, gs]
        y = _dot(c_g, state.astype(BF16)) * carry_in[:, gs]
        x_dt_g = x_dt[:, gs]
        mixes, inputs = [], []
        for r in range(heads_per_group):
            hh = g * heads_per_group + r
            seg = a_cs[:, hh:hh + 1] - a_cs_t[hh:hh + 1, :]
            decay = jnp.exp(jnp.where(causal, seg, -jnp.inf))
            mixes.append((cb * decay).astype(BF16))
            inputs.append(jnp.where(head_of_lane == r, x_dt_g, 0.0).astype(BF16))
        y = y + _dot(jnp.concatenate(mixes, axis=1), jnp.concatenate(inputs, axis=0))
        b_t = b_g.astype(F32).T.astype(BF16)
        state_ref[:, gs] = state * chunk_decay[:, gs] + _dot(b_t, x_end[:, gs])

        y = y + dskip_ref[:, gs] * x[:, gs]
        zg = z_ref[:, gs]
        y = y * (zg * jax.nn.sigmoid(zg))
        y = y * lax.rsqrt(jnp.mean(y * y, axis=-1, keepdims=True) + EPS) * gn_ref[:, gs]
        y_ref[:, gs] = y.astype(BF16)


def _ssd_core(xs, bm, cm, dt, z, a_log, d_skip, gnorm, expand):
    B, S, d_inner = xs.shape
    n_groups = SSD_N_GROUPS
    heads_per_group = d_inner // SSD_HEAD_DIM // n_groups
    bc_width = bm.shape[-1]
    L = SSD_CHUNK
    tok = lambda b, i: (b, i, 0)
    return pl.pallas_call(
        functools.partial(_ssd_core_kernel, n_groups=n_groups, heads_per_group=heads_per_group),
        grid=(B, S // L),
        in_specs=[
            pl.BlockSpec((None, L, d_inner), tok),
            pl.BlockSpec((None, L, bc_width), tok),
            pl.BlockSpec((None, L, bc_width), tok),
            pl.BlockSpec((None, L, LANES), tok),
            pl.BlockSpec((None, L, d_inner), tok),
            _resident((1, LANES)),
            _resident((1, d_inner)),
            _resident((1, d_inner)),
            _resident((LANES, d_inner)),
        ],
        out_specs=pl.BlockSpec((None, L, d_inner), tok),
        out_shape=jax.ShapeDtypeStruct((B, S, d_inner), BF16),
        scratch_shapes=[pltpu.VMEM((SSD_D_STATE, d_inner), F32)],
        compiler_params=_params("arbitrary", "arbitrary"),
        name="ssd_core",
    )(xs, bm, cm, dt, z, a_log, d_skip, gnorm, expand)


def kernel(x, mem, positions, ffn1_norm, ffn1_w_gate, ffn1_w_up, ffn1_w_down, mix_norm, da_w_in, da_lambda_q1, da_lambda_k1, da_lambda_q2, da_lambda_k2, da_subln, da_w_out, ssd_w_in, ssd_conv_w, ssd_conv_b, ssd_dt_bias, ssd_A_log, ssd_D, ssd_gnorm, ssd_w_out, xa_norm, xa_mem_norm, xa_w_q, xa_w_kv, xa_w_o, ffn2_norm, ffn2_w_gate, ffn2_w_up, ffn2_w_down, final_norm):
    B, S, D = x.shape
    depth = ffn1_norm.shape[0]
    T = B * S
    row = lambda v: v.reshape(1, -1).astype(F32)
    bf = lambda w: w.astype(BF16)

    inv = 1.0 / (ROPE_THETA ** (jnp.arange(0, DA_HEAD_DIM, 2, dtype=F32) / DA_HEAD_DIM))
    invf = jnp.tile(inv, LANES // inv.shape[0]).reshape(1, LANES)
    lane = jnp.arange(LANES)
    sign = jnp.where((lane % DA_HEAD_DIM) < DA_HEAD_DIM // 2, -1.0, 1.0).astype(F32).reshape(1, LANES)
    pos = positions.astype(F32).reshape(B, 1, S)

    mem_k, mem_v = _mem_kv(mem, xa_mem_norm.reshape(depth, 1, D), bf(xa_w_kv))

    h = x
    for i in range(depth):
        h = _ffn(h.reshape(T, D), row(ffn1_norm[i]), bf(ffn1_w_gate[i]), bf(ffn1_w_up[i]),
                 bf(ffn1_w_down[i]), row(final_norm), final=False).reshape(B, S, D)
        j = i // N_MIXERS
        if i % N_MIXERS == 0:
            tq = min(DA_Q_TILE, S)
            q, k, v = _da_qkv(h, pos, row(mix_norm[i]), bf(da_w_in[j]), invf, sign, tm=tq)
            lam_vecs = jnp.stack([da_lambda_q1[j], da_lambda_k1[j], da_lambda_q2[j], da_lambda_k2[j]]).astype(F32)
            mixed = _da_attn(q, k, v, lam_vecs, da_subln[j].astype(F32).reshape(LANES, 1),
                             lambda_init=_lambda_init(i), tq=tq)
            w_mix_out = bf(da_w_out[j])
        else:
            d_inner = ssd_w_out.shape[1]
            n_heads = ssd_A_log.shape[1]
            conv_dim = ssd_conv_w.shape[2]
            w_in = ssd_w_in[j]
            wdt = bf(jnp.pad(w_in[:, d_inner + conv_dim:], ((0, 0), (0, LANES - n_heads))))
            dt_bias = jnp.pad(ssd_dt_bias[j].astype(F32), (0, LANES - n_heads)).reshape(1, LANES)
            z, xs, bm, cm, dt = _ssd_in(h, row(mix_norm[i]), bf(w_in), wdt, ssd_conv_w[j].astype(F32),
                                        row(ssd_conv_b[j]), dt_bias, d_inner=d_inner)
            a_log = jnp.pad(ssd_A_log[j].astype(F32), (0, LANES - n_heads)).reshape(1, LANES)
            d_skip = jnp.repeat(ssd_D[j].astype(F32), SSD_HEAD_DIM).reshape(1, d_inner)
            expand = (jnp.arange(LANES)[:, None] == (jnp.arange(d_inner) // SSD_HEAD_DIM)[None, :]).astype(BF16)
            mixed = _ssd_core(xs, bm, cm, dt, z, a_log, d_skip, row(ssd_gnorm[j]), expand)
            w_mix_out = bf(ssd_w_out[j])
        h = _xattn(h, mixed, w_mix_out, row(xa_norm[i]), bf(xa_w_q[i]), mem_k[i], mem_v[i], bf(xa_w_o[i]))
        h = _ffn(h.reshape(T, D), row(ffn2_norm[i]), bf(ffn2_w_gate[i]), bf(ffn2_w_up[i]),
                 bf(ffn2_w_down[i]), row(final_norm), final=(i == depth - 1)).reshape(B, S, D)
    return h
```
